```python
import math
import functools
import jax
import jax.numpy as jnp
from jax import lax
import numpy as np

D_MODEL = 2048
BATCH = 8
SEQ = 2048
DEPTH = 1
DEC_BATCH = 32
DEC_SEQ = 4
PAST_LEN = 16384
PAGE_SIZE = 128

HEAD_DIM = 128
N_HEADS = D_MODEL // HEAD_DIM
H_A = N_HEADS // 2
HKV_A = max(1, H_A // 2)
H_B = N_HEADS - H_A
H_IDX = 16
D_IDX = 64
TOPK_MAX = 256
N_BUCKETS = 32
MAX_DISTANCE = 128
D_FF = ((8 * D_MODEL // 3 + 255) // 256) * 256
Q_BLOCK = 128
EPS = 1e-6
SCALE = HEAD_DIM ** -0.5
IDX_W_SCALE = (H_IDX * D_IDX) ** -0.5
W_A = H_A * HEAD_DIM
W_B = H_B * HEAD_DIM
SPLITS = (W_A, HKV_A * HEAD_DIM, HKV_A * HEAD_DIM, H_IDX * D_IDX, D_IDX, H_IDX, W_B, W_B, W_B)
D_IN = sum(SPLITS)
SPLIT_POINTS = tuple(int(v) for v in np.cumsum(SPLITS)[:-1])

kernel_name = 'hymba_dsa_stickbreak_decode_step'


def rmsnorm(x, g):
    xf = x.astype(jnp.float32)
    y = xf * lax.rsqrt(jnp.mean(xf * xf, axis=-1, keepdims=True) + EPS)
    return (y * g.astype(jnp.float32)).astype(x.dtype)


def t5_bucket(dist):
    max_exact = N_BUCKETS // 2
    d = jnp.maximum(dist, 0)
    df = jnp.maximum(d, 1).astype(jnp.float32)
    large = max_exact + (jnp.log(df / max_exact) / math.log(MAX_DISTANCE / max_exact)
                         * (N_BUCKETS - max_exact)).astype(jnp.int32)
    large = jnp.minimum(large, N_BUCKETS - 1)
    return jnp.where(d < max_exact, d, large)


def take_rows(a, idx):
    return jax.vmap(lambda a_b, i_b: a_b[i_b])(a, idx)


def project(h, w_in_l):
    B, T = h.shape[0], h.shape[1]
    z = jnp.einsum('btd,de->bte', h, w_in_l)
    q_a, k_a, v_a, q_i, k_i, w_i, q_b, k_b, v_b = jnp.split(z, list(SPLIT_POINTS), axis=-1)
    hd = (B, T, -1, HEAD_DIM)
    return (q_a.reshape(hd), k_a.reshape(hd), v_a.reshape(hd),
            q_i.reshape(B, T, H_IDX, D_IDX), k_i, w_i * IDX_W_SCALE,
            q_b.reshape(hd), k_b.reshape(hd), v_b.reshape(hd))


def indexer_scores(q_i, w_i, k_i):
    dots = jax.nn.relu(jnp.einsum('bthi,bsi->bths', q_i, k_i).astype(jnp.float32))
    return jnp.einsum('bth,bths->bts', w_i.astype(jnp.float32), dots)


def sparse_attend(q, k_sel, v_sel, idx, pos, rel_bias):
    B, T, K = idx.shape
    G = H_A // HKV_A
    qg = q.reshape(B, T, HKV_A, G, HEAD_DIM)
    logits = jnp.einsum('btkgd,btskd->btkgs', qg, k_sel).astype(jnp.float32) * SCALE
    dist = pos[None, :, None] - idx
    bias = rel_bias[t5_bucket(dist)].astype(jnp.float32)
    bias = bias.reshape(B, T, K, HKV_A, G).transpose(0, 1, 3, 4, 2)
    valid = (dist >= 0)[:, :, None, None, :]
    p = jax.nn.softmax(jnp.where(valid, logits + bias, -jnp.inf), axis=-1)
    out = jnp.einsum('btkgs,btskd->btkgd', p.astype(v_sel.dtype), v_sel)
    return out.reshape(B, T, W_A)


def dsa_prompt(q_a, k_a, v_a, q_i, k_i, w_i, rel_bias):
    B, S = q_a.shape[0], q_a.shape[1]
    topk = min(TOPK_MAX, S // 4)
    nb = S // Q_BLOCK
    key_pos = jnp.arange(S, dtype=jnp.int32)
    kv = jnp.stack([k_a, v_a], axis=2)

    def to_blocks(a):
        return a.reshape((B, nb, Q_BLOCK) + a.shape[2:]).swapaxes(0, 1)

    def block(args):
        q_blk, qi_blk, wi_blk, pos = args
        sc = indexer_scores(qi_blk, wi_blk, k_i)
        sc = jnp.where(key_pos[None, None, :] <= pos[None, :, None], sc, -jnp.inf)
        _, idx = lax.top_k(sc, topk)
        sel = take_rows(kv, idx.reshape(B, -1)).reshape(B, Q_BLOCK, topk, 2, HKV_A, HEAD_DIM)
        return sparse_attend(q_blk, sel[:, :, :, 0], sel[:, :, :, 1], idx, pos, rel_bias)

    out = lax.map(block, (to_blocks(q_a), to_blocks(q_i), to_blocks(w_i), key_pos.reshape(nb, Q_BLOCK)))
    return out.swapaxes(0, 1).reshape(B, S, W_A)


def dsa_sample(q_a, k_a, v_a, q_i, k_i, w_i, cache_kv_a, cache_idx, l, page_table, rel_bias):
    DB, T = q_a.shape[0], q_a.shape[1]
    P = page_table.shape[1] * PAGE_SIZE
    L = P + T
    topk = min(TOPK_MAX, L // 4)
    past_ki = cache_idx[l, page_table].reshape(DB, P, D_IDX)
    keys_i = jnp.concatenate([past_ki, k_i.astype(past_ki.dtype)], axis=1)
    pos = P + jnp.arange(T, dtype=jnp.int32)
    key_pos = jnp.arange(L, dtype=jnp.int32)
    sc = indexer_scores(q_i, w_i, keys_i)
    sc = jnp.where(key_pos[None, None, :] <= pos[None, :, None], sc, -jnp.inf)
    _, idx = lax.top_k(sc, topk)
    flat = idx.reshape(DB, -1)
    pidx = jnp.minimum(flat, P - 1)
    phys = jnp.take_along_axis(page_table, pidx // PAGE_SIZE, axis=1)
    past_rows = cache_kv_a[l, phys, pidx % PAGE_SIZE]
    kv_new = jnp.stack([k_a, v_a], axis=2)
    new_rows = take_rows(kv_new, jnp.clip(flat - P, 0, T - 1))
    sel = jnp.where((flat < P)[:, :, None, None, None], past_rows, new_rows.astype(past_rows.dtype))
    sel = sel.reshape(DB, T, topk, 2, HKV_A, HEAD_DIM)
    return sparse_attend(q_a, sel[:, :, :, 0], sel[:, :, :, 1], idx, pos, rel_bias)


def sb_block(z, mask, carry):
    last = z.ndim - 1
    lb1 = jnp.where(mask, jax.nn.log_sigmoid(-z), 0.0)
    suffix = lax.cumsum(lb1, axis=last, reverse=True) - lb1 + carry[..., None]
    a = jnp.where(mask, jnp.exp(jax.nn.log_sigmoid(z) + suffix), 0.0)
    return a, carry + jnp.sum(lb1, axis=-1)


def stick_breaking_prompt(q, k, v):
    B, S = q.shape[0], q.shape[1]
    nb = S // Q_BLOCK
    key_pos = jnp.arange(S, dtype=jnp.int32)
    q_blocks = q.reshape(B, nb, Q_BLOCK, H_B, HEAD_DIM).swapaxes(0, 1)

    def block(args):
        q_blk, pos = args
        z = jnp.einsum('bthd,bshd->bhts', q_blk, k).astype(jnp.float32) * SCALE
        mask = key_pos[None, :] < pos[:, None]
        a, _ = sb_block(z, mask, jnp.zeros(z.shape[:-1], jnp.float32))
        return jnp.einsum('bhts,bshd->bthd', a.astype(v.dtype), v)

    out = lax.map(block, (q_blocks, key_pos.reshape(nb, Q_BLOCK)))
    return out.swapaxes(0, 1).reshape(B, S, W_B)


def stick_breaking_sample(q, k_new, v_new, cache_kv_b, l, page_table):
    DB, T = q.shape[0], q.shape[1]
    pos = jnp.arange(T, dtype=jnp.int32)
    z = jnp.einsum('bthd,bshd->bhts', q, k_new).astype(jnp.float32) * SCALE
    a, carry = sb_block(z, pos[None, :] < pos[:, None], jnp.zeros(z.shape[:-1], jnp.float32))
    acc = jnp.einsum('bhts,bshd->bthd', a.astype(v_new.dtype), v_new).astype(jnp.float32)

    def step(c, phys):
        carry_c, acc_c = c
        kv = cache_kv_b[l, phys]
        zp = jnp.einsum('bthd,bshd->bhts', q.astype(kv.dtype), kv[:, :, 0]).astype(jnp.float32) * SCALE
        ap, carry_c = sb_block(zp, jnp.ones(zp.shape[-2:], dtype=bool), carry_c)
        acc_c = acc_c + jnp.einsum('bhts,bshd->bthd', ap.astype(kv.dtype), kv[:, :, 1]).astype(jnp.float32)
        return (carry_c, acc_c), None

    (_, acc), _ = lax.scan(step, (carry, acc), page_table.T, reverse=True)
    return acc.astype(q.dtype).reshape(DB, T, W_B)


def merge_groups(o_a, o_b, g_grp_a_l, g_grp_b_l, w_out_l):
    o = jnp.concatenate([rmsnorm(o_a, g_grp_a_l), rmsnorm(o_b, g_grp_b_l)], axis=-1)
    return jnp.einsum('btd,de->bte', o, w_out_l)


def mixer_prompt(h, w_in_l, g_grp_a_l, g_grp_b_l, w_out_l, rel_bias):
    q_a, k_a, v_a, q_i, k_i, w_i, q_b, k_b, v_b = project(h, w_in_l)
    o_a = dsa_prompt(q_a, k_a, v_a, q_i, k_i, w_i, rel_bias)
    o_b = stick_breaking_prompt(q_b, k_b, v_b)
    o = merge_groups(o_a, o_b, g_grp_a_l, g_grp_b_l, w_out_l)
    return o, (jnp.stack([k_a, v_a], axis=2), k_i, jnp.stack([k_b, v_b], axis=2))


def mixer_sample(h, cache_kv_a, cache_idx, cache_kv_b, page_table, l, w_in_l, g_grp_a_l, g_grp_b_l, w_out_l, rel_bias):
    q_a, k_a, v_a, q_i, k_i, w_i, q_b, k_b, v_b = project(h, w_in_l)
    o_a = dsa_sample(q_a, k_a, v_a, q_i, k_i, w_i, cache_kv_a, cache_idx, l, page_table, rel_bias)
    o_b = stick_breaking_sample(q_b, k_b, v_b, cache_kv_b, l, page_table)
    o = merge_groups(o_a, o_b, g_grp_a_l, g_grp_b_l, w_out_l)
    return o, (jnp.stack([k_a, v_a], axis=2), k_i, jnp.stack([k_b, v_b], axis=2))


def trunk_layer(x, c, mixer, w_ada_l, b_ada_l, g_pre_mix_l, g_post_mix_l, g_pre_ffn_l, w_gate_up_l, w_down_l, g_post_ffn_l):
    mod = jnp.einsum('bd,de->be', jax.nn.silu(c), w_ada_l) + b_ada_l
    sh_m, sc_m, gt_m, sh_f, sc_f, gt_f = jnp.split(mod[:, None, :], 6, axis=-1)
    h = rmsnorm(x, g_pre_mix_l) * (1.0 + sc_m) + sh_m
    o, new_state = mixer(h)
    x = x + gt_m * rmsnorm(o, g_post_mix_l)
    h = rmsnorm(x, g_pre_ffn_l) * (1.0 + sc_f) + sh_f
    gate, up = jnp.split(jnp.einsum('btd,df->btf', h, w_gate_up_l), 2, axis=-1)
    f = jnp.einsum('btf,fd->btd', jax.nn.silu(gate) * up, w_down_l)
    x = x + gt_f * rmsnorm(f, g_post_ffn_l)
    return x, new_state


def setup_inputs(seed: int = 0) -> dict:
    key = jax.random.key(seed)
    ks = jax.random.split(key, 24)
    n_pages = PAST_LEN // PAGE_SIZE
    n_pool = (DEC_BATCH * n_pages * 5) // 4
    nrm = jax.random.normal
    f32 = jnp.float32
    page_table = jax.random.permutation(ks[5], n_pool)[:DEC_BATCH * n_pages]
    page_table = page_table.reshape(DEC_BATCH, n_pages).astype(jnp.int32)
    return {
        'x_prompt': nrm(ks[0], (BATCH, SEQ, D_MODEL), f32),
        'x_sample': nrm(ks[1], (DEC_BATCH, DEC_SEQ, D_MODEL), f32),
        'cache_kv_a': nrm(ks[2], (DEPTH, n_pool, PAGE_SIZE, 2, HKV_A, HEAD_DIM), f32),
        'cache_idx': nrm(ks[3], (DEPTH, n_pool, PAGE_SIZE, D_IDX), f32),
        'cache_kv_b': nrm(ks[4], (DEPTH, n_pool, PAGE_SIZE, 2, H_B, HEAD_DIM), f32),
        'page_table': page_table,
        'c_prompt': nrm(ks[6], (BATCH, D_MODEL), f32),
        'c_sample': nrm(ks[7], (DEC_BATCH, D_MODEL), f32),
        'rel_bias': 0.5 * nrm(ks[8], (N_BUCKETS, H_A), f32),
        'w_ada': 0.5 * D_MODEL ** -0.5 * nrm(ks[9], (DEPTH, D_MODEL, 6 * D_MODEL), f32),
        'b_ada': 0.02 * nrm(ks[10], (DEPTH, 6 * D_MODEL), f32),
        'g_pre_mix': 1.0 + 0.05 * nrm(ks[11], (DEPTH, D_MODEL), f32),
        'w_in': D_MODEL ** -0.5 * nrm(ks[12], (DEPTH, D_MODEL, D_IN), f32),
        'g_grp_a': 1.0 + 0.05 * nrm(ks[13], (DEPTH, W_A), f32),
        'g_grp_b': 1.0 + 0.05 * nrm(ks[14], (DEPTH, W_B), f32),
        'w_out': D_MODEL ** -0.5 * nrm(ks[15], (DEPTH, D_MODEL, D_MODEL), f32),
        'g_post_mix': 1.0 + 0.05 * nrm(ks[16], (DEPTH, D_MODEL), f32),
        'g_pre_ffn': 1.0 + 0.05 * nrm(ks[17], (DEPTH, D_MODEL), f32),
        'w_gate_up': D_MODEL ** -0.5 * nrm(ks[18], (DEPTH, D_MODEL, 2 * D_FF), f32),
        'w_down': D_FF ** -0.5 * nrm(ks[19], (DEPTH, D_FF, D_MODEL), f32),
        'g_post_ffn': 1.0 + 0.05 * nrm(ks[20], (DEPTH, D_MODEL), f32),
    }


def reference(x_prompt, x_sample, cache_kv_a, cache_idx, cache_kv_b, page_table, c_prompt, c_sample,
              rel_bias, w_ada, b_ada, g_pre_mix, w_in, g_grp_a, g_grp_b, w_out, g_post_mix,
              g_pre_ffn, w_gate_up, w_down, g_post_ffn):
    y_prompt, y_sample = x_prompt, x_sample
    p_states, s_states = [], []
    for l in range(DEPTH):
        mix_p = functools.partial(mixer_prompt, w_in_l=w_in[l], g_grp_a_l=g_grp_a[l],
                                  g_grp_b_l=g_grp_b[l], w_out_l=w_out[l], rel_bias=rel_bias)
        mix_s = functools.partial(mixer_sample, cache_kv_a=cache_kv_a, cache_idx=cache_idx,
                                  cache_kv_b=cache_kv_b, page_table=page_table, l=l, w_in_l=w_in[l],
                                  g_grp_a_l=g_grp_a[l], g_grp_b_l=g_grp_b[l], w_out_l=w_out[l],
                                  rel_bias=rel_bias)
        y_prompt, sp = trunk_layer(y_prompt, c_prompt, mix_p, w_ada[l], b_ada[l], g_pre_mix[l], g_post_mix[l],
                                   g_pre_ffn[l], w_gate_up[l], w_down[l], g_post_ffn[l])
        y_sample, ss = trunk_layer(y_sample, c_sample, mix_s, w_ada[l], b_ada[l], g_pre_mix[l], g_post_mix[l],
                                   g_pre_ffn[l], w_gate_up[l], w_down[l], g_post_ffn[l])
        p_states.append(sp)
        s_states.append(ss)
    new_kv_a_prompt = jnp.stack([s[0] for s in p_states], axis=0)
    new_idx_prompt = jnp.stack([s[1] for s in p_states], axis=0)
    new_kv_b_prompt = jnp.stack([s[2] for s in p_states], axis=0)
    new_kv_a_sample = jnp.stack([s[0] for s in s_states], axis=0)
    new_idx_sample = jnp.stack([s[1] for s in s_states], axis=0)
    new_kv_b_sample = jnp.stack([s[2] for s in s_states], axis=0)
    return (y_prompt, y_sample, new_kv_a_prompt, new_idx_prompt, new_kv_b_prompt,
            new_kv_a_sample, new_idx_sample, new_kv_b_sample)
```

```python
import functools
import math

import jax
import jax.numpy as jnp
import numpy as np
from jax import lax
from jax.experimental import pallas as pl
from jax.experimental.pallas import tpu as pltpu

F32 = jnp.float32
BF16 = jnp.bfloat16
I32 = jnp.int32

HEAD_DIM = 128
H_A = 8
HKV_A = 4
H_B = 8
H_IDX = 16
D_IDX = 64
TOPK_MAX = 256
N_BUCKETS = 32
MAX_DISTANCE = 128
PAGE = 128
EPS = 1e-6
SCALE = HEAD_DIM ** -0.5
IDX_W_SCALE = (H_IDX * D_IDX) ** -0.5
W_A = H_A * HEAD_DIM
KV_A = HKV_A * HEAD_DIM
W_B = H_B * HEAD_DIM
INT_MIN = -2 ** 31
INT_MAX = 2 ** 31 - 1
SB_DEAD = -120.0

LANES = 128
VMEM_LIMIT = 56 * 1024 * 1024


def _cparams(n_axes, vmem=VMEM_LIMIT):
    return pltpu.CompilerParams(dimension_semantics=("arbitrary",) * n_axes,
                                vmem_limit_bytes=vmem)


def _bucket_thresholds():
    max_exact = N_BUCKETS // 2
    d = np.arange(0, 4 * MAX_DISTANCE, dtype=np.int32)
    df = np.maximum(d, 1).astype(np.float32)
    large = max_exact + (np.log(df / np.float32(max_exact)) / np.float32(math.log(MAX_DISTANCE / max_exact))
                         * np.float32(N_BUCKETS - max_exact)).astype(np.int32)
    large = np.minimum(large, N_BUCKETS - 1)
    bucket = np.where(d < max_exact, d, large)
    assert bucket[-1] == N_BUCKETS - 1 and np.all(np.diff(bucket) >= 0)
    return [int(np.argmax(bucket >= b)) for b in range(1, N_BUCKETS)]


_BUCKET_THR = _bucket_thresholds()


def _rel_bias_of(dist, rb_ref, h):
    val = jnp.full(dist.shape, rb_ref[0, h], F32)
    for b in range(1, N_BUCKETS):
        val = jnp.where(dist >= _BUCKET_THR[b - 1], rb_ref[b, h], val)
    return val


def _float_key(x):
    bits = lax.bitcast_convert_type(x, I32)
    return bits ^ ((bits >> 31) & INT_MAX)


def _ceil_avg(lo, hi):
    x = lo ^ hi
    return (lo & hi) + (x >> 1) + (x & 1)


def _rms(x, g):
    return x * lax.rsqrt(jnp.mean(x * x, axis=-1, keepdims=True) + EPS) * g


def _mod_kernel(c_ref, w_ref, b_ref, o_ref):
    c = c_ref[...]
    a = c * jax.nn.sigmoid(c)
    o_ref[...] = jnp.dot(a, w_ref[...], preferred_element_type=F32,
                         precision=lax.Precision.HIGHEST) + b_ref[...]


def _modulation(c_all, w_ada, b_ada):
    r, d = c_all.shape
    n = w_ada.shape[1]
    tn = 1024
    return pl.pallas_call(
        _mod_kernel,
        grid=(n // tn,),
        in_specs=[pl.BlockSpec((r, d), lambda i: (0, 0)),
                  pl.BlockSpec((d, tn), lambda i: (0, i)),
                  pl.BlockSpec((1, tn), lambda i: (0, i))],
        out_specs=pl.BlockSpec((r, tn), lambda i: (0, i)),
        out_shape=jax.ShapeDtypeStruct((r, n), F32),
        compiler_params=_cparams(1),
        name="adaln_mod",
    )(c_all, w_ada, b_ada)


N_MAIN = 6


def _proj_kernel(x_ref, g_ref, sh_ref, sc_ref, wm_ref, ws_ref, wst_ref,
                 qa_ref, kva_ref, kvab_ref, qi_ref, qb_ref, kvb_ref, kvbb_ref,
                 sm_ref, kt_ref, kta_ref, ktb_ref, h_ref, *, kc):
    n = pl.program_id(1)
    tm = x_ref.shape[0]

    @pl.when(n == 0)
    def _():
        y = _rms(x_ref[...], g_ref[...])
        h_ref[...] = (y * (1.0 + sc_ref[...]) + sh_ref[...]).astype(BF16)

    def mm():
        return jnp.dot(h_ref[...], wm_ref[...], preferred_element_type=F32)

    @pl.when(n == 0)
    def _():
        qa_ref[...] = mm().astype(BF16)

    @pl.when(n == 1)
    def _():
        r = mm()
        kva_ref[...] = r
        kvab_ref[...] = r.astype(BF16)

    @pl.when(n == 2)
    def _():
        qi_ref[...] = mm().astype(BF16)

    @pl.when(n == 3)
    def _():
        qb_ref[...] = mm().astype(BF16)

    @pl.when((n == 4) | (n == 5))
    def _():
        r = mm()
        kvb_ref[...] = r
        kvbb_ref[...] = r.astype(BF16)

    @pl.when(n == N_MAIN)
    def _():
        r = jnp.dot(h_ref[...], ws_ref[...], preferred_element_type=F32)
        lane = lax.broadcasted_iota(I32, r.shape, 1)
        is_w = (lane >= D_IDX) & (lane < D_IDX + H_IDX)
        sm_ref[...] = jnp.where(is_w, r * IDX_W_SCALE, r)
        rt = lax.dot_general(wst_ref[...], h_ref[...], (((1,), (1,)), ((), ())),
                             preferred_element_type=F32)
        kt = rt[:D_IDX]
        kt_ref[...] = kt
        ktb = kt.astype(BF16)
        z = jnp.zeros_like(ktb)
        for u in range(tm // kc):
            sl = slice(u * kc, (u + 1) * kc)
            kta_ref[u] = jnp.concatenate([ktb[:, sl], z[:, sl]], axis=0)
            ktb_ref[u] = jnp.concatenate([z[:, sl], ktb[:, sl]], axis=0)


def _project(x2d, g_pre, shift, scale, w_main, w_small, w_small_t, *, tm, rows_per_group, kc):
    m, d = x2d.shape
    nb = m // tm
    bpg = rows_per_group // tm
    r = shift.shape[1]
    ng = m // rows_per_group
    nkc = rows_per_group // kc
    cpb = tm // kc
    grid = (nb, N_MAIN + 1)
    row = lambda i, n: (i, 0)
    grp = lambda i, n: (i // bpg, 0, 0)
    out_shape = [
        jax.ShapeDtypeStruct((m, W_A), BF16),
        jax.ShapeDtypeStruct((m, 2 * KV_A), F32),
        jax.ShapeDtypeStruct((m, 2 * KV_A), BF16),
        jax.ShapeDtypeStruct((m, H_IDX * D_IDX), BF16),
        jax.ShapeDtypeStruct((m, W_B), BF16),
        jax.ShapeDtypeStruct((m, 2 * W_B), F32),
        jax.ShapeDtypeStruct((m, 2 * W_B), BF16),
        jax.ShapeDtypeStruct((m, LANES), F32),
        jax.ShapeDtypeStruct((ng, D_IDX, rows_per_group), F32),
        jax.ShapeDtypeStruct((ng, nkc, 2 * D_IDX, kc), BF16),
        jax.ShapeDtypeStruct((ng, nkc, 2 * D_IDX, kc), BF16),
    ]
    kvb_map = lambda i, n: (i, n // (N_MAIN - 1))
    ktmap = lambda i, n: (i // bpg, 0, i % bpg)
    kcmap = lambda i, n: (i // bpg, i % bpg, 0, 0)
    out_specs = [
        pl.BlockSpec((tm, W_A), row),
        pl.BlockSpec((tm, 2 * KV_A), row),
        pl.BlockSpec((tm, 2 * KV_A), row),
        pl.BlockSpec((tm, H_IDX * D_IDX), row),
        pl.BlockSpec((tm, W_B), row),
        pl.BlockSpec((tm, W_B), kvb_map),
        pl.BlockSpec((tm, W_B), kvb_map),
        pl.BlockSpec((tm, LANES), row),
        pl.BlockSpec((None, D_IDX, tm), ktmap),
        pl.BlockSpec((None, cpb, 2 * D_IDX, kc), kcmap),
        pl.BlockSpec((None, cpb, 2 * D_IDX, kc), kcmap),
    ]
    in_specs = [
        pl.BlockSpec((tm, d), row),
        pl.BlockSpec((1, d), lambda i, n: (0, 0)),
        pl.BlockSpec((None, r, d), grp),
        pl.BlockSpec((None, r, d), grp),
        pl.BlockSpec((d, 1024), lambda i, n: (0, jnp.minimum(n, N_MAIN - 1))),
        pl.BlockSpec((d, LANES), lambda i, n: (0, 0)),
        pl.BlockSpec((LANES, d), lambda i, n: (0, 0)),
    ]
    return pl.pallas_call(
        functools.partial(_proj_kernel, kc=kc),
        grid=grid, in_specs=in_specs, out_specs=out_specs, out_shape=out_shape,
        scratch_shapes=[pltpu.VMEM((tm, d), BF16)],
        compiler_params=_cparams(2),
        name="prenorm_proj",
    )(x2d, g_pre, shift, scale, w_main, w_small, w_small_t)


def _bias_tiles_kernel(rb_ref, o_ref):
    h = pl.program_id(0)
    tq = o_ref.shape[-1]
    row = lax.broadcasted_iota(I32, (tq, tq), 0)
    col = lax.broadcasted_iota(I32, (tq, tq), 1)
    for k in range(2):
        dist = row - col + k * tq
        val = jnp.full(dist.shape, rb_ref[0, h], F32)
        for b in range(1, N_BUCKETS):
            val = jnp.where(dist >= _BUCKET_THR[b - 1], rb_ref[b, h], val)
        o_ref[k] = val


def _bias_tiles(rel_bias, tq):
    return pl.pallas_call(
        _bias_tiles_kernel,
        grid=(H_A,),
        in_specs=[pl.BlockSpec(memory_space=pltpu.SMEM)],
        out_specs=pl.BlockSpec((2, None, tq, tq), lambda h: (0, h, 0, 0)),
        out_shape=jax.ShapeDtypeStruct((2, H_A, tq, tq), F32),
        compiler_params=_cparams(1),
        name="t5_bias_tiles",
    )(rel_bias)


def _dsa_prompt_kernel(rb_ref, qi_ref, sm_ref, kta_ref, ktb_ref, qa_ref, kv_ref, bias_ref,
                       o_ref, keys_ref, wb_ref, m_ref, l_ref, acc_ref, *, topk):
    j = pl.program_id(1)
    tq = qa_ref.shape[0]
    nkc = j + 1
    row = j * tq + lax.broadcasted_iota(I32, (tq, 1), 0)

    w = sm_ref[:, D_IDX:D_IDX + H_IDX]
    for h in range(H_IDX):
        wb_ref[h] = jnp.broadcast_to(w[:, h:h + 1], (tq, tq))

    def score_chunk(c, carry):
        ka = kta_ref[c]
        kb = ktb_ref[c]
        acc = jnp.zeros((tq, tq), F32)
        for hp in range(H_IDX // 2):
            qp = qi_ref[:, hp * LANES:(hp + 1) * LANES]
            da = jnp.dot(qp, ka, preferred_element_type=F32)
            db = jnp.dot(qp, kb, preferred_element_type=F32)
            acc = acc + wb_ref[2 * hp] * jnp.maximum(da, 0.0) + wb_ref[2 * hp + 1] * jnp.maximum(db, 0.0)
        col = c * tq + lax.broadcasted_iota(I32, (1, tq), 1)
        keys_ref[c] = jnp.where(col <= row, _float_key(acc), INT_MIN)
        return carry

    lax.fori_loop(0, nkc, score_chunk, 0)

    need = (row + 1) > topk

    def count_ge(mid):
        midb = jnp.broadcast_to(mid, (tq, LANES))

        def body(c, cnt):
            kc = keys_ref[c]
            for u in range(tq // LANES):
                cnt = cnt + jnp.where(kc[:, u * LANES:(u + 1) * LANES] >= midb, 1, 0)
            return cnt

        cnt = lax.fori_loop(0, nkc, body, jnp.zeros((tq, LANES), I32))
        return jnp.sum(cnt, axis=1, keepdims=True)

    def search_cond(s):
        it, lo, hi = s
        return (it < 40) & (jnp.max(jnp.where(lo < hi, 1, 0)) > 0)

    def search_body(s):
        it, lo, hi = s
        mid = _ceil_avg(lo, hi)
        c = count_ge(mid)
        ge = c >= topk
        lo2 = jnp.where(ge, mid, lo)
        hi2 = jnp.where(c == topk, mid, jnp.where(ge, hi, mid - 1))
        return it + 1, lo2, hi2

    lo0 = jnp.full((tq, 1), INT_MIN + 1, I32)
    hi0 = jnp.where(need, INT_MAX, INT_MIN + 1)
    _, thr, _ = lax.while_loop(search_cond, search_body, (jnp.int32(0), lo0, hi0))
    thrb = jnp.broadcast_to(thr, (tq, tq))

    far = N_BUCKETS - 1
    top = lax.broadcasted_iota(I32, (2 * tq, 1), 0) < tq
    for g in range(HKV_A):
        h0, h1 = 2 * g, 2 * g + 1
        q2 = jnp.concatenate([qa_ref[:, h0 * HEAD_DIM:(h0 + 1) * HEAD_DIM],
                              qa_ref[:, h1 * HEAD_DIM:(h1 + 1) * HEAD_DIM]], axis=0)
        m_ref[...] = jnp.full(m_ref.shape, -jnp.inf, F32)
        l_ref[...] = jnp.zeros(l_ref.shape, F32)
        acc_ref[...] = jnp.zeros(acc_ref.shape, F32)

        def block(c, bias):
            off = pl.multiple_of(c * tq, tq)
            k = kv_ref[pl.ds(off, tq), g * HEAD_DIM:(g + 1) * HEAD_DIM]
            v = kv_ref[pl.ds(off, tq), KV_A + g * HEAD_DIM:KV_A + (g + 1) * HEAD_DIM]
            s = lax.dot_general(q2, k, (((1,), (1,)), ((), ())), preferred_element_type=F32) * SCALE + bias
            sel = keys_ref[c] >= thrb
            s = jnp.where(jnp.concatenate([sel, sel], axis=0), s, -jnp.inf)
            m_old = m_ref[...]
            m_new = jnp.maximum(m_old, jnp.max(s, axis=1, keepdims=True))
            m_safe = jnp.where(m_new == -jnp.inf, 0.0, m_new)
            p = jnp.exp(s - m_safe)
            alpha = jnp.exp(m_old - m_safe)
            l_ref[...] = alpha * l_ref[...] + jnp.sum(p, axis=1, keepdims=True)
            acc_ref[...] = alpha * acc_ref[...] + jnp.dot(p.astype(BF16), v, preferred_element_type=F32)
            m_ref[...] = m_new

        far_bias = jnp.where(top, rb_ref[far, h0], rb_ref[far, h1])

        def far_block(c, carry):
            block(c, far_bias)
            return carry

        lax.fori_loop(0, j - 1, far_block, 0)

        @pl.when(j >= 1)
        def _():
            block(j - 1, jnp.concatenate([bias_ref[1, h0], bias_ref[1, h1]], axis=0))

        block(j, jnp.concatenate([bias_ref[0, h0], bias_ref[0, h1]], axis=0))

        out = acc_ref[...] / l_ref[...]
        o_ref[:, h0 * HEAD_DIM:(h0 + 1) * HEAD_DIM] = out[:tq]
        o_ref[:, h1 * HEAD_DIM:(h1 + 1) * HEAD_DIM] = out[tq:]


def _dsa_prompt(rel_bias, qi, sm, kta, ktb, qa, kvab, bias_tiles, *, nb, s, tq, topk):
    nq = s // tq
    m = nb * s
    qrow = lambda b, j, rb: (b * nq + j, 0)
    in_specs = [
        pl.BlockSpec((tq, H_IDX * D_IDX), qrow),
        pl.BlockSpec((tq, LANES), qrow),
        pl.BlockSpec((None, nq, 2 * D_IDX, tq), lambda b, j, rb: (b, 0, 0, 0)),
        pl.BlockSpec((None, nq, 2 * D_IDX, tq), lambda b, j, rb: (b, 0, 0, 0)),
        pl.BlockSpec((tq, W_A), qrow),
        pl.BlockSpec((s, 2 * KV_A), lambda b, j, rb: (b, 0)),
        pl.BlockSpec((2, H_A, tq, tq), lambda b, j, rb: (0, 0, 0, 0)),
    ]
    return pl.pallas_call(
        functools.partial(_dsa_prompt_kernel, topk=topk),
        grid_spec=pltpu.PrefetchScalarGridSpec(
            num_scalar_prefetch=1, grid=(nb, nq), in_specs=in_specs,
            out_specs=pl.BlockSpec((tq, W_A), qrow),
            scratch_shapes=[pltpu.VMEM((nq, tq, tq), I32),
                            pltpu.VMEM((H_IDX, tq, tq), F32),
                            pltpu.VMEM((2 * tq, 1), F32),
                            pltpu.VMEM((2 * tq, 1), F32),
                            pltpu.VMEM((2 * tq, HEAD_DIM), F32)]),
        out_shape=jax.ShapeDtypeStruct((m, W_A), F32),
        compiler_params=_cparams(2),
        name="dsa_prompt",
    )(rel_bias, qi, sm, kta, ktb, qa, kvab, bias_tiles)


def _sb_block(q, k, v, u_strict, mask, carry):
    z = lax.dot_general(q, k, (((1,), (1,)), ((), ())), preferred_element_type=F32) * SCALE
    l1p = jnp.log1p(jnp.exp(-jnp.abs(z)))
    ls_pos = jnp.minimum(z, 0.0) - l1p
    ls_neg = jnp.minimum(-z, 0.0) - l1p
    lb1 = ls_neg if mask is None else jnp.where(mask, ls_neg, 0.0)
    hi = lb1.astype(BF16)
    lo = (lb1 - hi.astype(F32)).astype(BF16)
    suffix = (jnp.dot(hi, u_strict, preferred_element_type=F32)
              + jnp.dot(lo, u_strict, preferred_element_type=F32)) + carry
    a = jnp.exp(ls_pos + suffix)
    if mask is not None:
        a = jnp.where(mask, a, 0.0)
    contrib = jnp.dot(a.astype(BF16), v, preferred_element_type=F32)
    return contrib, carry + jnp.sum(lb1, axis=1, keepdims=True)


def _sb_prompt_kernel(q_ref, k_ref, v_ref, u_ref, o_ref, acc_ref, carry_ref):
    j = pl.program_id(2)
    tq = q_ref.shape[0]
    q = q_ref[...]
    u = u_ref[...]
    row = lax.broadcasted_iota(I32, (tq, tq), 0)
    col = lax.broadcasted_iota(I32, (tq, tq), 1)

    contrib, carry = _sb_block(q, k_ref[pl.ds(pl.multiple_of(j * tq, tq), tq), :],
                               v_ref[pl.ds(pl.multiple_of(j * tq, tq), tq), :], u, col < row,
                               jnp.zeros((tq, 1), F32))
    acc_ref[...] = contrib
    carry_ref[...] = carry

    def cond(c):
        return (c >= 0) & (jnp.max(carry_ref[...]) > SB_DEAD)

    def body(c):
        off = pl.multiple_of(c * tq, tq)
        contrib, carry = _sb_block(q, k_ref[pl.ds(off, tq), :], v_ref[pl.ds(off, tq), :], u, None,
                                   carry_ref[...])
        acc_ref[...] += contrib
        carry_ref[...] = carry
        return c - 1

    lax.while_loop(cond, body, j - 1)
    o_ref[...] = acc_ref[...]


def _sb_prompt(qb, kvbb, u_strict, *, nb, s, tq):
    nq = s // tq
    m = nb * s
    return pl.pallas_call(
        _sb_prompt_kernel,
        grid=(nb, H_B, nq),
        in_specs=[pl.BlockSpec((tq, HEAD_DIM), lambda b, h, j: (b * nq + j, h)),
                  pl.BlockSpec((s, HEAD_DIM), lambda b, h, j: (b, h)),
                  pl.BlockSpec((s, HEAD_DIM), lambda b, h, j: (b, H_B + h)),
                  pl.BlockSpec((tq, tq), lambda b, h, j: (0, 0))],
        out_specs=pl.BlockSpec((tq, HEAD_DIM), lambda b, h, j: (b * nq + j, h)),
        out_shape=jax.ShapeDtypeStruct((m, W_B), F32),
        scratch_shapes=[pltpu.VMEM((tq, HEAD_DIM), F32), pltpu.VMEM((tq, 1), F32)],
        compiler_params=_cparams(3),
        name="sb_prompt",
    )(qb, kvbb, kvbb, u_strict)


def _merge_kernel(oa_ref, ob_ref, ga_ref, gb_ref, wo_ref, x_ref, gpost_ref, gate_ref,
                  gpre_ref, sh_ref, sc_ref, x1_ref, h2_ref):
    na = _rms(oa_ref[...], ga_ref[...]).astype(BF16)
    nb = _rms(ob_ref[...], gb_ref[...]).astype(BF16)
    o = (jnp.dot(na, wo_ref[:W_A, :], preferred_element_type=F32)
         + jnp.dot(nb, wo_ref[W_A:, :], preferred_element_type=F32))
    x1 = x_ref[...] + gate_ref[...] * _rms(o, gpost_ref[...])
    x1_ref[...] = x1
    h2_ref[...] = (_rms(x1, gpre_ref[...]) * (1.0 + sc_ref[...]) + sh_ref[...]).astype(BF16)


def _merge(oa, ob, g_a, g_b, w_out, x2d, g_post, gate, g_pre_ffn, shift, scale, *, tm, rows_per_group):
    m, d = x2d.shape
    bpg = rows_per_group // tm
    r = gate.shape[1]
    row = lambda i: (i, 0)
    grp = lambda i: (i // bpg, 0, 0)
    one = lambda i: (0, 0)
    return pl.pallas_call(
        _merge_kernel,
        grid=(m // tm,),
        in_specs=[pl.BlockSpec((tm, W_A), row), pl.BlockSpec((tm, W_B), row),
                  pl.BlockSpec((1, W_A), one), pl.BlockSpec((1, W_B), one),
                  pl.BlockSpec((W_A + W_B, d), one),
                  pl.BlockSpec((tm, d), row), pl.BlockSpec((1, d), one),
                  pl.BlockSpec((None, r, d), grp), pl.BlockSpec((1, d), one),
                  pl.BlockSpec((None, r, d), grp), pl.BlockSpec((None, r, d), grp)],
        out_specs=[pl.BlockSpec((tm, d), row), pl.BlockSpec((tm, d), row)],
        out_shape=[jax.ShapeDtypeStruct((m, d), F32), jax.ShapeDtypeStruct((m, d), BF16)],
        compiler_params=_cparams(1),
        name="merge_outproj",
    )(oa, ob, g_a, g_b, w_out, x2d, g_post, gate, g_pre_ffn, shift, scale)


def _ffn_kernel(h_ref, wg_ref, wu_ref, wd_ref, x1_ref, gpost_ref, gate_ref, y_ref, acc_ref):
    f = pl.program_id(1)
    h = h_ref[...]
    g = jnp.dot(h, wg_ref[...], preferred_element_type=F32)
    u = jnp.dot(h, wu_ref[...], preferred_element_type=F32)
    act = (g * jax.nn.sigmoid(g) * u).astype(BF16)
    part = jnp.dot(act, wd_ref[...], preferred_element_type=F32)

    @pl.when(f == 0)
    def _():
        acc_ref[...] = part

    @pl.when(f > 0)
    def _():
        acc_ref[...] += part

    @pl.when(f == pl.num_programs(1) - 1)
    def _():
        y_ref[...] = x1_ref[...] + gate_ref[...] * _rms(acc_ref[...], gpost_ref[...])


def _ffn(h2, w_gu, w_down, x1, g_post, gate, *, tm, tf, rows_per_group):
    m, d = x1.shape
    ff = w_down.shape[0]
    nf = ff // tf
    bpg = rows_per_group // tm
    r = gate.shape[1]
    return pl.pallas_call(
        _ffn_kernel,
        grid=(m // tm, nf),
        in_specs=[pl.BlockSpec((tm, d), lambda i, f: (i, 0)),
                  pl.BlockSpec((d, tf), lambda i, f: (0, f)),
                  pl.BlockSpec((d, tf), lambda i, f: (0, nf + f)),
                  pl.BlockSpec((tf, d), lambda i, f: (f, 0)),
                  pl.BlockSpec((tm, d), lambda i, f: (i, 0)),
                  pl.BlockSpec((1, d), lambda i, f: (0, 0)),
                  pl.BlockSpec((None, r, d), lambda i, f: (i // bpg, 0, 0))],
        out_specs=pl.BlockSpec((tm, d), lambda i, f: (i, 0)),
        out_shape=jax.ShapeDtypeStruct((m, d), F32),
        scratch_shapes=[pltpu.VMEM((tm, d), F32)],
        compiler_params=_cparams(2),
        name="swiglu_ffn",
    )(h2, w_gu, w_gu, w_down, x1, g_post, gate)


def _page_copy(src_ref, buf_ref, sem_ref, pt_ref, b, p, slot):
    return pltpu.make_async_copy(src_ref.at[pt_ref[b, p]], buf_ref.at[slot, p], sem_ref.at[slot])


def _sample_select_kernel(pt_ref, cidx_ref, q_ref, wsel_ref, knew_ref, lstrict_ref, ut_ref, eye_ref,
                          idx_ref, k3_ref, pbuf_ref, sem_ref, *, topk, t_new):
    b = pl.program_id(0)
    nb = pl.num_programs(0)
    n_pages = pbuf_ref.shape[1]
    n_rows = k3_ref.shape[1]
    slot = b % 2
    group = 8

    def start_all(bb, sl):
        def body(p, c):
            _page_copy(cidx_ref, pbuf_ref, sem_ref, pt_ref, bb, p, sl).start()
            return c
        lax.fori_loop(0, n_pages, body, 0)

    @pl.when(b == 0)
    def _():
        start_all(0, 0)

    @pl.when(b + 1 < nb)
    def _():
        start_all(b + 1, 1 - slot)

    def wait_body(p, c):
        _page_copy(cidx_ref, pbuf_ref, sem_ref, pt_ref, b, p, slot).wait()
        return c
    lax.fori_loop(0, n_pages, wait_body, 0)

    q = q_ref[...]
    wsel = wsel_ref[...]
    trow = lax.broadcasted_iota(I32, (8, LANES), 0)

    def to_keys(kt):
        dots = jnp.maximum(jnp.dot(q, kt, preferred_element_type=F32), 0.0)
        sc = jnp.dot(wsel, dots.astype(BF16), preferred_element_type=F32)
        return _float_key(sc)

    def score_group(gi, c):
        base = gi * group
        kt = jnp.concatenate([pbuf_ref[slot, base + u] for u in range(group)], axis=1).astype(BF16)
        keys = to_keys(kt)
        for t in range(t_new):
            for u in range(group):
                k3_ref[t, pl.ds(base + u, 1), :] = keys[t:t + 1, u * LANES:(u + 1) * LANES]
        return c
    lax.fori_loop(0, n_pages // group, score_group, 0)

    knew = to_keys(knew_ref[...])
    lane = lax.broadcasted_iota(I32, (8, LANES), 1)
    knew = jnp.where((lane <= trow) & (lane < t_new), knew, INT_MIN)
    pad = jnp.full((n_rows - n_pages - 1, LANES), INT_MIN, I32)
    for t in range(t_new):
        k3_ref[t, n_pages:n_pages + 1, :] = knew[t:t + 1]
        k3_ref[t, n_pages + 1:, :] = pad

    def search_cond(s):
        it = s[0]
        live = s[1] < s[1 + t_new]
        for t in range(1, t_new):
            live = live | (s[1 + t] < s[1 + t_new + t])
        return (it < 40) & live

    def search_body(s):
        it = s[0]
        los, his = [], []
        for t in range(t_new):
            lo, hi = s[1 + t], s[1 + t_new + t]
            mid = _ceil_avg(lo, hi)
            c = jnp.sum(jnp.where(k3_ref[t] >= mid, 1, 0))
            ge = c >= topk
            los.append(jnp.where(ge, mid, lo))
            his.append(jnp.where(c == topk, mid, jnp.where(ge, hi, mid - 1)))
        return (it + 1, *los, *his)

    init = (jnp.int32(0),) + (jnp.int32(INT_MIN + 1),) * t_new + (jnp.int32(INT_MAX),) * t_new
    res = lax.while_loop(search_cond, search_body, init)

    rowid = lax.broadcasted_iota(I32, (n_rows, 1), 0).astype(F32)
    qlane = lax.broadcasted_iota(I32, (1, 2 * LANES), 1).astype(F32)
    liota = lax.broadcasted_iota(I32, (LANES, 2 * LANES), 0).astype(F32)
    lstrict = lstrict_ref[...]
    idx_ref[...] = jnp.zeros(idx_ref.shape, I32)
    for t in range(t_new):
        thr = res[1 + t]
        sel = jnp.where(k3_ref[t] >= thr, 1.0, 0.0)
        selb = sel.astype(BF16)
        cnt = jnp.sum(sel, axis=1, keepdims=True)
        start = jnp.dot(lstrict, jnp.broadcast_to(cnt, (n_rows, LANES)).astype(BF16),
                        preferred_element_type=F32)[:, :1]
        onehot = jnp.where((start <= qlane) & (qlane < start + cnt), 1.0, 0.0)
        nt = (((1,), (1,)), ((), ()))
        et = lax.dot_general(ut_ref[...], selb, nt, preferred_element_type=F32)
        selt = lax.dot_general(eye_ref[...], selb, nt, preferred_element_type=F32)
        ept = jnp.where(selt > 0.5, et, -1.0).astype(BF16)
        gt = jnp.dot(ept, onehot.astype(BF16), preferred_element_type=F32)
        start_q = jnp.sum(onehot * start, axis=0, keepdims=True)
        chunk_q = jnp.sum(onehot * rowid, axis=0, keepdims=True)
        lane_q = jnp.sum(jnp.where(gt == qlane - start_q, liota, 0.0), axis=0, keepdims=True)
        total = jnp.sum(cnt)
        ids = (chunk_q * LANES + lane_q).astype(I32)
        idx_ref[t:t + 1, :] = jnp.where(qlane < total, ids, -1)


def _sample_select(page_table, cidx_t, q_s, wsel, knew_t, *, topk, t_new):
    db, n_pages = page_table.shape
    n_rows = ((n_pages + 1 + LANES - 1) // LANES) * LANES
    lstrict = jnp.asarray(np.tril(np.ones((n_rows, n_rows), np.float32), -1), BF16)
    ut = jnp.asarray(np.tril(np.ones((LANES, LANES), np.float32), -1), BF16)
    eye = jnp.asarray(np.eye(LANES, dtype=np.float32), BF16)
    const = lambda shape: pl.BlockSpec(shape, lambda b, pt: (0,) * len(shape))
    return pl.pallas_call(
        functools.partial(_sample_select_kernel, topk=topk, t_new=t_new),
        grid_spec=pltpu.PrefetchScalarGridSpec(
            num_scalar_prefetch=1, grid=(db,),
            in_specs=[pl.BlockSpec(memory_space=pl.ANY),
                      pl.BlockSpec((None,) + q_s.shape[1:], lambda b, pt: (b, 0, 0)),
                      pl.BlockSpec((None,) + wsel.shape[1:], lambda b, pt: (b, 0, 0)),
                      pl.BlockSpec((None,) + knew_t.shape[1:], lambda b, pt: (b, 0, 0)),
                      const((n_rows, n_rows)), const((LANES, LANES)), const((LANES, LANES))],
            out_specs=pl.BlockSpec((None, 8, 2 * LANES), lambda b, pt: (b, 0, 0)),
            scratch_shapes=[pltpu.VMEM((t_new, n_rows, LANES), I32),
                            pltpu.VMEM((2, n_pages, D_IDX, PAGE), F32),
                            pltpu.SemaphoreType.DMA((2,))]),
        out_shape=jax.ShapeDtypeStruct((db, 8, 2 * LANES), I32),
        compiler_params=_cparams(1),
        name="sample_select",
    )(page_table, cidx_t, q_s, wsel, knew_t, lstrict, ut, eye)


def _sample_attend_kernel(idx_s_ref, pt_ref, rb_ref, kva_ref, kvnew_ref, idxv_ref, q_ref, o_ref,
                          gbuf_ref, sem_ref, *, t_new, n_past):
    b = pl.program_id(0)
    nsel = idxv_ref.shape[1]
    rows = 2 * HKV_A

    def copies(t, qi):
        i = idx_s_ref[b * t_new + t, qi]
        i = jnp.maximum(i, 0)
        dst = gbuf_ref.at[t, pl.ds(qi * rows, rows)]
        past_row = pt_ref[b, jnp.minimum(i, n_past - 1) // PAGE] * PAGE + (i % PAGE)
        new_row = b * t_new + jnp.clip(i - n_past, 0, t_new - 1)
        return (i < n_past,
                pltpu.make_async_copy(kva_ref.at[pl.ds(past_row * rows, rows)], dst, sem_ref.at[0]),
                pltpu.make_async_copy(kvnew_ref.at[pl.ds(new_row * rows, rows)], dst, sem_ref.at[0]))

    def issue(t):
        def body(qi, c):
            is_past, cp_past, cp_new = copies(t, qi)

            @pl.when(is_past)
            def _():
                cp_past.start()

            @pl.when(jnp.logical_not(is_past))
            def _():
                cp_new.start()
            return c
        lax.fori_loop(0, nsel, body, 0)

    def drain(t):
        def body(qi, c):
            pltpu.make_async_copy(kva_ref.at[pl.ds(0, rows)], gbuf_ref.at[t, pl.ds(qi * rows, rows)],
                                  sem_ref.at[0]).wait()
            return c
        lax.fori_loop(0, nsel, body, 0)

    for t in range(t_new):
        issue(t)
    for t in range(t_new):
        drain(t)

    lane = lax.broadcasted_iota(I32, (1, nsel), 1)
    for t in range(t_new):
        ids = idxv_ref[t:t + 1, :]
        valid = ids >= 0
        dist = (n_past + t) - ids
        for g in range(HKV_A):
            k = gbuf_ref[t, pl.ds(g, nsel, stride=rows), :].astype(BF16)
            v = gbuf_ref[t, pl.ds(HKV_A + g, nsel, stride=rows), :].astype(BF16)
            qg = q_ref[t, g]
            s = lax.dot_general(qg, k, (((1,), (1,)), ((), ())), preferred_element_type=F32) * SCALE
            b0 = _rel_bias_of(dist, rb_ref, 2 * g)
            b1 = _rel_bias_of(dist, rb_ref, 2 * g + 1)
            srow = lax.broadcasted_iota(I32, s.shape, 0)
            s = s + jnp.where(srow == 0, b0, b1)
            s = jnp.where(valid, s, -jnp.inf)
            p = jnp.exp(s - jnp.max(s, axis=1, keepdims=True))
            p = p / jnp.sum(p, axis=1, keepdims=True)
            out = jnp.dot(p.astype(BF16), v, preferred_element_type=F32)
            o_ref[t:t + 1, (2 * g) * HEAD_DIM:(2 * g + 1) * HEAD_DIM] = out[0:1]
            o_ref[t:t + 1, (2 * g + 1) * HEAD_DIM:(2 * g + 2) * HEAD_DIM] = out[1:2]
    if t_new < 8:
        o_ref[t_new:, :] = jnp.zeros((8 - t_new, W_A), F32)


def _sample_attend(idx, page_table, rel_bias, kva_rows, kvnew_rows, q_pad, *, t_new, n_past):
    db = page_table.shape[0]
    nsel = idx.shape[-1]
    idx_s = idx[:, :t_new].reshape(db * t_new, nsel)
    return pl.pallas_call(
        functools.partial(_sample_attend_kernel, t_new=t_new, n_past=n_past),
        grid_spec=pltpu.PrefetchScalarGridSpec(
            num_scalar_prefetch=3, grid=(db,),
            in_specs=[pl.BlockSpec(memory_space=pl.ANY), pl.BlockSpec(memory_space=pl.ANY),
                      pl.BlockSpec((None, 8, nsel), lambda b, *_: (b, 0, 0)),
                      pl.BlockSpec((None,) + q_pad.shape[1:], lambda b, *_: (b, 0, 0, 0, 0))],
            out_specs=pl.BlockSpec((None, 8, W_A), lambda b, *_: (b, 0, 0)),
            scratch_shapes=[pltpu.VMEM((t_new, nsel * 2 * HKV_A, HEAD_DIM), F32),
                            pltpu.SemaphoreType.DMA((1,))]),
        out_shape=jax.ShapeDtypeStruct((db, 8, W_A), F32),
        compiler_params=_cparams(1),
        name="sample_attend",
    )(idx_s, page_table, rel_bias, kva_rows, kvnew_rows, idx, q_pad)


def _sb_sample_kernel(pt_ref, kvb_ref, kvnew_ref, q_ref, u_ref, o_ref, pbuf_ref, acc_ref, carry_ref,
                      sem_ref, *, t_new):
    b = pl.program_id(0)
    n_pages = pt_ref.shape[1]
    prow = pbuf_ref.shape[1]
    step = 2 * H_B
    u = u_ref[...]
    trow = lax.broadcasted_iota(I32, (8, PAGE), 0)
    scol = lax.broadcasted_iota(I32, (8, PAGE), 1)
    live_rows = lax.broadcasted_iota(I32, (8, 1), 0) < t_new

    def page_copy(p, slot):
        return pltpu.make_async_copy(kvb_ref.at[pl.ds(pt_ref[b, p] * prow, prow)], pbuf_ref.at[slot],
                                     sem_ref.at[slot])

    def process(slot, mask):
        for h in range(H_B):
            k = pbuf_ref[slot, pl.ds(h, PAGE, stride=step), :].astype(BF16)
            v = pbuf_ref[slot, pl.ds(H_B + h, PAGE, stride=step), :].astype(BF16)
            contrib, carry = _sb_block(q_ref[h], k, v, u, mask, carry_ref[h])
            acc_ref[h] += contrib
            carry_ref[h] = carry

    page_copy(n_pages - 1, 0).start()
    new_rows = t_new * step
    cp_new = pltpu.make_async_copy(kvnew_ref.at[pl.ds(b * new_rows, new_rows)],
                                   pbuf_ref.at[1, pl.ds(0, new_rows)], sem_ref.at[1])
    cp_new.start()
    pbuf_ref[1, pl.ds(new_rows, prow - new_rows), :] = jnp.zeros((prow - new_rows, HEAD_DIM), F32)
    acc_ref[...] = jnp.zeros(acc_ref.shape, F32)
    carry_ref[...] = jnp.zeros(carry_ref.shape, F32)
    cp_new.wait()
    process(1, (scol < trow) & (scol < t_new))

    def alive():
        c = jnp.where(live_rows, carry_ref[...], -jnp.inf)
        return jnp.max(c) > SB_DEAD

    def cond(s):
        i, go = s
        return (i < n_pages) & go

    def body(s):
        i, _ = s
        slot = i % 2
        page_copy(n_pages - 1 - i, slot).wait()

        @pl.when(i + 1 < n_pages)
        def _():
            page_copy(n_pages - 2 - i, 1 - slot).start()
        process(slot, None)
        return i + 1, alive()

    i_end, _ = lax.while_loop(cond, body, (jnp.int32(0), alive()))

    @pl.when(i_end < n_pages)
    def _():
        page_copy(n_pages - 1 - i_end, i_end % 2).wait()

    for h in range(H_B):
        o_ref[:, h * HEAD_DIM:(h + 1) * HEAD_DIM] = acc_ref[h]


def _sb_sample(page_table, kvb_rows, kvnew_rows, q_pad, u_strict, *, t_new):
    db = page_table.shape[0]
    prow = PAGE * 2 * H_B
    return pl.pallas_call(
        functools.partial(_sb_sample_kernel, t_new=t_new),
        grid_spec=pltpu.PrefetchScalarGridSpec(
            num_scalar_prefetch=1, grid=(db,),
            in_specs=[pl.BlockSpec(memory_space=pl.ANY), pl.BlockSpec(memory_space=pl.ANY),
                      pl.BlockSpec((None, H_B, 8, HEAD_DIM), lambda b, pt: (b, 0, 0, 0)),
                      pl.BlockSpec((PAGE, PAGE), lambda b, pt: (0, 0))],
            out_specs=pl.BlockSpec((None, 8, W_B), lambda b, pt: (b, 0, 0)),
            scratch_shapes=[pltpu.VMEM((2, prow, HEAD_DIM), F32),
                            pltpu.VMEM((H_B, 8, HEAD_DIM), F32),
                            pltpu.VMEM((H_B, 8, 1), F32),
                            pltpu.SemaphoreType.DMA((2,))]),
        out_shape=jax.ShapeDtypeStruct((db, 8, W_B), F32),
        compiler_params=_cparams(1),
        name="sb_sample",
    )(page_table, kvb_rows, kvnew_rows, q_pad, u_strict)


def _strict_upper(n):
    return jnp.asarray(np.tril(np.ones((n, n), np.float32), -1), BF16)


def kernel(x_prompt, x_sample, cache_kv_a, cache_idx, cache_kv_b, page_table, c_prompt, c_sample, rel_bias, w_ada, b_ada, g_pre_mix, w_in, g_grp_a, g_grp_b, w_out, g_post_mix, g_pre_ffn, w_gate_up, w_down, g_post_ffn):
    nb, s, d = x_prompt.shape
    db, t_new, _ = x_sample.shape
    n_pages = page_table.shape[1]
    n_past = n_pages * PAGE
    assert w_ada.shape[0] == 1 and d == (H_A + H_B) * HEAD_DIM
    assert cache_kv_a.shape[2:] == (PAGE, 2, HKV_A, HEAD_DIM) and t_new <= 8

    sp = np.cumsum([0, W_A, KV_A, KV_A, H_IDX * D_IDX, D_IDX, H_IDX, W_B, W_B, W_B])
    wi = w_in[0]
    col = lambda i: wi[:, sp[i]:sp[i + 1]]
    w_main = jnp.concatenate([col(0), col(1), col(2), col(3), col(6), col(7), col(8)], axis=1).astype(BF16)
    w_small = jnp.concatenate([col(4), col(5), jnp.zeros((d, LANES - D_IDX - H_IDX), F32)], axis=1).astype(BF16)
    w_small_t = w_small.T
    w_out_b = w_out[0].astype(BF16)
    w_gu_b = w_gate_up[0].astype(BF16)
    w_down_b = w_down[0].astype(BF16)

    mod = _modulation(jnp.concatenate([c_prompt, c_sample], axis=0), w_ada[0], b_ada)
    mod_p = mod[:nb].reshape(nb, 6, 1, d)
    mod_s = jnp.repeat(mod[nb:], t_new, axis=0).reshape(1, db * t_new, 6, d)
    mp = [mod_p[:, i] for i in range(6)]
    ms = [mod_s[:, :, i] for i in range(6)]

    tq = 256
    topk_p = min(TOPK_MAX, s // 4)
    xp = x_prompt.reshape(nb * s, d)
    (qa, kva, kvab, qi, qb, kvb, kvbb, sm, kt, kta, ktb) = _project(
        xp, g_pre_mix, mp[0], mp[1], w_main, w_small, w_small_t, tm=512, rows_per_group=s, kc=tq)
    bias_tiles = _bias_tiles(rel_bias, tq)
    oa = _dsa_prompt(rel_bias, qi, sm, kta, ktb, qa, kvab, bias_tiles, nb=nb, s=s, tq=tq, topk=topk_p)
    ob = _sb_prompt(qb, kvbb, _strict_upper(tq), nb=nb, s=s, tq=tq)
    x1, h2 = _merge(oa, ob, g_grp_a, g_grp_b, w_out_b, xp, g_post_mix, mp[2], g_pre_ffn, mp[3], mp[4],
                    tm=256, rows_per_group=s)
    y_prompt = _ffn(h2, w_gu_b, w_down_b, x1, g_post_ffn, mp[5], tm=512, tf=512, rows_per_group=s)

    ms_rows = db * t_new
    topk_s = min(TOPK_MAX, (n_past + t_new) // 4)
    xs = x_sample.reshape(ms_rows, d)
    (qa_s, kva_s, _, qi_s, qb_s, kvb_s, _, sm_s, kt_s, _, _) = _project(
        xs, g_pre_mix, ms[0], ms[1], w_main, w_small, w_small_t, tm=ms_rows, rows_per_group=ms_rows,
        kc=ms_rows)

    q_s = qi_s.reshape(db, t_new * H_IDX, D_IDX)
    w_s = sm_s[:, D_IDX:D_IDX + H_IDX].reshape(db, t_new, H_IDX)
    wsel = (jnp.eye(8, t_new, dtype=F32)[None, :, :, None] * w_s[:, None]).reshape(db, 8, t_new * H_IDX)
    knew_t = jnp.pad(kt_s[0].reshape(D_IDX, db, t_new).transpose(1, 0, 2),
                     ((0, 0), (0, 0), (0, LANES - t_new))).astype(BF16)
    cidx_t = jnp.swapaxes(cache_idx[0], 1, 2)
    idx = _sample_select(page_table, cidx_t, q_s, wsel.astype(BF16), knew_t, topk=topk_s, t_new=t_new)

    kva_rows = cache_kv_a.reshape(-1, HEAD_DIM)
    kvnew_rows = kva_s.reshape(-1, HEAD_DIM)
    qa4 = qa_s.reshape(db, t_new, HKV_A, H_A // HKV_A, HEAD_DIM)
    q_pad = jnp.pad(qa4, ((0, 0), (0, 0), (0, 0), (0, 8 - H_A // HKV_A), (0, 0)))
    oa_s = _sample_attend(idx, page_table, rel_bias, kva_rows, kvnew_rows, q_pad, t_new=t_new, n_past=n_past)

    kvb_rows = cache_kv_b.reshape(-1, HEAD_DIM)
    kvbnew_rows = kvb_s.reshape(-1, HEAD_DIM)
    qb_pad = jnp.pad(qb_s.reshape(db, t_new, H_B, HEAD_DIM).transpose(0, 2, 1, 3),
                     ((0, 0), (0, 0), (0, 8 - t_new), (0, 0)))
    ob_s = _sb_sample(page_table, kvb_rows, kvbnew_rows, qb_pad, _strict_upper(PAGE), t_new=t_new)

    oa_s = oa_s[:, :t_new].reshape(ms_rows, W_A)
    ob_s = ob_s[:, :t_new].reshape(ms_rows, W_B)
    x1_s, h2_s = _merge(oa_s, ob_s, g_grp_a, g_grp_b, w_out_b, xs, g_post_mix, ms[2], g_pre_ffn, ms[3],
                        ms[4], tm=ms_rows, rows_per_group=ms_rows)
    y_sample = _ffn(h2_s, w_gu_b, w_down_b, x1_s, g_post_ffn, ms[5], tm=ms_rows, tf=512,
                    rows_per_group=ms_rows)

    return (y_prompt.reshape(nb, s, d),
            y_sample.reshape(db, t_new, d),
            kva.reshape(1, nb, s, 2, HKV_A, HEAD_DIM),
            jnp.swapaxes(kt, 1, 2)[None],
            kvb.reshape(1, nb, s, 2, H_B, HEAD_DIM),
            kva_s.reshape(1, db, t_new, 2, HKV_A, HEAD_DIM),
            sm_s[:, :D_IDX].reshape(1, db, t_new, D_IDX),
            kvb_s.reshape(1, db, t_new, 2, H_B, HEAD_DIM))
```

```python
import functools
import math

import jax
import jax.numpy as jnp
import numpy as np
from jax import lax
from jax.experimental import pallas as pl
from jax.experimental.pallas import tpu as pltpu

F32 = jnp.float32
BF16 = jnp.bfloat16
I32 = jnp.int32

HEAD_DIM = 128
H_A = 8
HKV_A = 4
H_B = 8
H_IDX = 16
D_IDX = 64
TOPK_MAX = 256
N_BUCKETS = 32
MAX_DISTANCE = 128
PAGE = 128
EPS = 1e-6
SCALE = HEAD_DIM ** -0.5
IDX_W_SCALE = (H_IDX * D_IDX) ** -0.5
W_A = H_A * HEAD_DIM
KV_A = HKV_A * HEAD_DIM
W_B = H_B * HEAD_DIM
INT_MIN = -2 ** 31
INT_MAX = 2 ** 31 - 1
SB_DEAD = -120.0

LANES = 128
VMEM_LIMIT = 56 * 1024 * 1024


def _cparams(n_axes, vmem=VMEM_LIMIT):
    return pltpu.CompilerParams(dimension_semantics=("arbitrary",) * n_axes,
                                vmem_limit_bytes=vmem)


def _bucket_thresholds():
    max_exact = N_BUCKETS // 2
    d = np.arange(0, 4 * MAX_DISTANCE, dtype=np.int32)
    df = np.maximum(d, 1).astype(np.float32)
    large = max_exact + (np.log(df / np.float32(max_exact)) / np.float32(math.log(MAX_DISTANCE / max_exact))
                         * np.float32(N_BUCKETS - max_exact)).astype(np.int32)
    large = np.minimum(large, N_BUCKETS - 1)
    bucket = np.where(d < max_exact, d, large)
    assert bucket[-1] == N_BUCKETS - 1 and np.all(np.diff(bucket) >= 0)
    return [int(np.argmax(bucket >= b)) for b in range(1, N_BUCKETS)]


_BUCKET_THR = _bucket_thresholds()


def _rel_bias_of(dist, rb_ref, h):
    val = jnp.full(dist.shape, rb_ref[0, h], F32)
    for b in range(1, N_BUCKETS):
        val = jnp.where(dist >= _BUCKET_THR[b - 1], rb_ref[b, h], val)
    return val


def _float_key(x):
    bits = lax.bitcast_convert_type(x, I32)
    return bits ^ ((bits >> 31) & INT_MAX)


def _ceil_avg(lo, hi):
    x = lo ^ hi
    return (lo & hi) + (x >> 1) + (x & 1)


def _rms(x, g):
    return x * lax.rsqrt(jnp.mean(x * x, axis=-1, keepdims=True) + EPS) * g


def _mod_kernel(c_ref, w_ref, b_ref, o_ref):
    c = c_ref[...]
    a = c * jax.nn.sigmoid(c)
    o_ref[...] = jnp.dot(a, w_ref[...], preferred_element_type=F32,
                         precision=lax.Precision.HIGHEST) + b_ref[...]


def _modulation(c_all, w_ada, b_ada):
    r, d = c_all.shape
    n = w_ada.shape[1]
    tn = 1024
    return pl.pallas_call(
        _mod_kernel,
        grid=(n // tn,),
        in_specs=[pl.BlockSpec((r, d), lambda i: (0, 0)),
                  pl.BlockSpec((d, tn), lambda i: (0, i)),
                  pl.BlockSpec((1, tn), lambda i: (0, i))],
        out_specs=pl.BlockSpec((r, tn), lambda i: (0, i)),
        out_shape=jax.ShapeDtypeStruct((r, n), F32),
        compiler_params=_cparams(1),
        name="adaln_mod",
    )(c_all, w_ada, b_ada)


PROJ_TN = 1024
_PROJ_GROUPS = {"qa": (0, 1), "kva": (1, 1), "qi": (2, 1), "qb": (3, 1), "kvb": (4, 2)}
N_MAIN = 6


def _proj_kernel(x_ref, g_ref, sh_ref, sc_ref, wm_ref, ws_ref, wst_ref,
                 qa_ref, kva_ref, kvab_ref, qi_ref, qb_ref, kvb_ref, kvbb_ref,
                 sm_ref, kt_ref, h_ref):
    n = pl.program_id(1)

    @pl.when(n == 0)
    def _():
        y = _rms(x_ref[...], g_ref[...])
        h_ref[...] = (y * (1.0 + sc_ref[...]) + sh_ref[...]).astype(BF16)

    def mm():
        return jnp.dot(h_ref[...], wm_ref[...], preferred_element_type=F32)

    def in_group(name):
        first, count = _PROJ_GROUPS[name]
        return (n >= first) & (n < first + count)

    @pl.when(in_group("qa"))
    def _():
        qa_ref[...] = mm().astype(BF16)

    @pl.when(in_group("kva"))
    def _():
        r = mm()
        kva_ref[...] = r
        kvab_ref[...] = r.astype(BF16)

    @pl.when(in_group("qi"))
    def _():
        qi_ref[...] = mm().astype(BF16)

    @pl.when(in_group("qb"))
    def _():
        qb_ref[...] = mm().astype(BF16)

    @pl.when(in_group("kvb"))
    def _():
        r = mm()
        kvb_ref[...] = r
        kvbb_ref[...] = r.astype(BF16)

    @pl.when(n == N_MAIN)
    def _():
        r = jnp.dot(h_ref[...], ws_ref[...], preferred_element_type=F32)
        lane = lax.broadcasted_iota(I32, r.shape, 1)
        is_w = (lane >= D_IDX) & (lane < D_IDX + H_IDX)
        sm_ref[...] = jnp.where(is_w, r * IDX_W_SCALE, r)
        rt = lax.dot_general(wst_ref[...], h_ref[...], (((1,), (1,)), ((), ())),
                             preferred_element_type=F32)
        kt_ref[...] = rt[:D_IDX]


def _project(x2d, g_pre, shift, scale, w_main, w_small, w_small_t, *, tm, rows_per_group):
    m, d = x2d.shape
    nb = m // tm
    bpg = rows_per_group // tm
    r = shift.shape[1]
    ng = m // rows_per_group
    tn = PROJ_TN
    grid = (nb, N_MAIN + 1)
    row = lambda i, n: (i, 0)
    grp = lambda i, n: (i // bpg, 0, 0)

    def tile(name):
        first, count = _PROJ_GROUPS[name]
        return pl.BlockSpec((tm, tn), lambda i, n: (i, jnp.clip(n - first, 0, count - 1)))

    out_shape = [
        jax.ShapeDtypeStruct((m, W_A), BF16),
        jax.ShapeDtypeStruct((m, 2 * KV_A), F32),
        jax.ShapeDtypeStruct((m, 2 * KV_A), BF16),
        jax.ShapeDtypeStruct((m, H_IDX * D_IDX), BF16),
        jax.ShapeDtypeStruct((m, W_B), BF16),
        jax.ShapeDtypeStruct((m, 2 * W_B), F32),
        jax.ShapeDtypeStruct((m, 2 * W_B), BF16),
        jax.ShapeDtypeStruct((m, LANES), F32),
        jax.ShapeDtypeStruct((ng, D_IDX, rows_per_group), F32),
    ]
    out_specs = [
        tile("qa"), tile("kva"), tile("kva"), tile("qi"), tile("qb"), tile("kvb"), tile("kvb"),
        pl.BlockSpec((tm, LANES), row),
        pl.BlockSpec((None, D_IDX, tm), lambda i, n: (i // bpg, 0, i % bpg)),
    ]
    in_specs = [
        pl.BlockSpec((tm, d), row),
        pl.BlockSpec((1, d), lambda i, n: (0, 0)),
        pl.BlockSpec((None, r, d), grp),
        pl.BlockSpec((None, r, d), grp),
        pl.BlockSpec((d, tn), lambda i, n: (0, jnp.minimum(n, N_MAIN - 1))),
        pl.BlockSpec((d, LANES), lambda i, n: (0, 0)),
        pl.BlockSpec((LANES, d), lambda i, n: (0, 0)),
    ]
    return pl.pallas_call(
        _proj_kernel,
        grid=grid, in_specs=in_specs, out_specs=out_specs, out_shape=out_shape,
        scratch_shapes=[pltpu.VMEM((tm, d), BF16)],
        compiler_params=_cparams(2),
        name="prenorm_proj",
    )(x2d, g_pre, shift, scale, w_main, w_small, w_small_t)


def _bias_tiles_kernel(rb_ref, o_ref):
    h = pl.program_id(0)
    tq = o_ref.shape[-1]
    key = lax.broadcasted_iota(I32, (tq, tq), 0)
    qry = lax.broadcasted_iota(I32, (tq, tq), 1)
    for k in range(2):
        o_ref[k] = _rel_bias_of(qry - key + k * tq, rb_ref, h)


def _bias_tiles(rel_bias, tq):
    return pl.pallas_call(
        _bias_tiles_kernel,
        grid=(H_A,),
        in_specs=[pl.BlockSpec(memory_space=pltpu.SMEM)],
        out_specs=pl.BlockSpec((2, None, tq, tq), lambda h: (0, h, 0, 0)),
        out_shape=jax.ShapeDtypeStruct((2, H_A, tq, tq), F32),
        compiler_params=_cparams(1),
        name="t5_bias_tiles",
    )(rel_bias)


def _dsa_prompt_kernel(rb_ref, qi_ref, smq_ref, smk_ref, qa_ref, kv_ref, bias_ref,
                       o_ref, keys_ref, kab_ref, vt_ref, qit_ref, qat_ref, m_ref, l_ref, acc_ref, *, topk):
    j = pl.program_id(1)
    tq = qa_ref.shape[0]
    nq = keys_ref.shape[0]
    nkc = j + 1
    qpos = j * tq + lax.broadcasted_iota(I32, (1, tq), 1)

    @pl.when(j == 0)
    def _():
        def prep(c, carry):
            off = pl.multiple_of(c * tq, tq)
            r = smk_ref[pl.ds(off, tq), :]
            lane = lax.broadcasted_iota(I32, r.shape, 1)
            ka = jnp.where(lane < D_IDX, r, 0.0)
            kb = jnp.where(lane >= D_IDX, pltpu.roll(r, D_IDX, axis=1), 0.0)
            kab_ref[c] = jnp.concatenate([ka, kb], axis=0).astype(BF16)
            for g in range(HKV_A):
                v = kv_ref[pl.ds(off, tq), KV_A + g * HEAD_DIM:KV_A + (g + 1) * HEAD_DIM]
                vt_ref[c, g] = v.astype(F32).T.astype(BF16)
            return carry

        lax.fori_loop(0, nq, prep, 0)

    for hp in range(H_IDX // 2):
        qit_ref[hp] = qi_ref[:, hp * LANES:(hp + 1) * LANES].astype(F32).T.astype(BF16)
    for h in range(H_A):
        qat_ref[h] = qa_ref[:, h * HEAD_DIM:(h + 1) * HEAD_DIM].astype(F32).T.astype(BF16)
    wt = smq_ref[...].T[D_IDX:D_IDX + H_IDX]

    def score_chunk(c, carry):
        kab = kab_ref[c]
        acc = jnp.zeros((tq, tq), F32)
        for hp in range(H_IDX // 2):
            d = jnp.maximum(jnp.dot(kab, qit_ref[hp], preferred_element_type=F32), 0.0)
            acc = acc + wt[2 * hp:2 * hp + 1] * d[:tq] + wt[2 * hp + 1:2 * hp + 2] * d[tq:]
        kpos = c * tq + lax.broadcasted_iota(I32, (tq, 1), 0)
        keys_ref[c] = jnp.where(kpos <= qpos, _float_key(acc), INT_MIN)
        return carry

    lax.fori_loop(0, nkc, score_chunk, 0)

    need = (qpos + 1) > topk

    def count_ge(mid):
        def body(c, cnt):
            hit = jnp.where(keys_ref[c] >= mid, 1, 0)
            return cnt + jnp.sum(hit.reshape(tq // 8, 8, tq), axis=0)

        cnt = lax.fori_loop(0, nkc, body, jnp.zeros((8, tq), I32))
        return jnp.sum(cnt, axis=0, keepdims=True)

    def search_body(_, s):
        lo, hi = s
        mid = _ceil_avg(lo, hi)
        c = count_ge(mid)
        ge = c >= topk
        return jnp.where(ge, mid, lo), jnp.where(c == topk, mid, jnp.where(ge, hi, mid - 1))

    lo0 = jnp.full((1, tq), INT_MIN + 1, I32)
    hi0 = jnp.where(need, INT_MAX, INT_MIN + 1)
    thr, _ = lax.fori_loop(0, 32, search_body, (lo0, hi0))

    far = N_BUCKETS - 1
    left = lax.broadcasted_iota(I32, (1, 2 * tq), 1) < tq
    for g in range(HKV_A):
        h0, h1 = 2 * g, 2 * g + 1
        qt2 = jnp.concatenate([qat_ref[h0], qat_ref[h1]], axis=1)
        m_ref[...] = jnp.full(m_ref.shape, -jnp.inf, F32)
        l_ref[...] = jnp.zeros(l_ref.shape, F32)
        acc_ref[...] = jnp.zeros(acc_ref.shape, F32)

        def block(c, bias):
            off = pl.multiple_of(c * tq, tq)
            k = kv_ref[pl.ds(off, tq), g * HEAD_DIM:(g + 1) * HEAD_DIM]
            s = jnp.dot(k, qt2, preferred_element_type=F32) * SCALE + bias
            sel = keys_ref[c] >= thr
            s = jnp.where(jnp.concatenate([sel, sel], axis=1), s, -jnp.inf)
            m_old = m_ref[...]
            m_new = jnp.maximum(m_old, jnp.max(s, axis=0, keepdims=True))
            m_safe = jnp.where(m_new == -jnp.inf, 0.0, m_new)
            p = jnp.exp(s - m_safe)
            alpha = jnp.exp(m_old - m_safe)
            l_ref[...] = alpha * l_ref[...] + jnp.sum(p, axis=0, keepdims=True)
            acc_ref[...] = alpha * acc_ref[...] + jnp.dot(vt_ref[c, g], p.astype(BF16),
                                                          preferred_element_type=F32)
            m_ref[...] = m_new

        far_bias = jnp.where(left, rb_ref[far, h0], rb_ref[far, h1])

        def far_block(c, carry):
            block(c, far_bias)
            return carry

        lax.fori_loop(0, j - 1, far_block, 0)

        @pl.when(j >= 1)
        def _():
            block(j - 1, jnp.concatenate([bias_ref[1, h0], bias_ref[1, h1]], axis=1))

        block(j, jnp.concatenate([bias_ref[0, h0], bias_ref[0, h1]], axis=1))

        out = acc_ref[...] / l_ref[...]
        o_ref[:, h0 * HEAD_DIM:(h0 + 1) * HEAD_DIM] = out[:, :tq].T
        o_ref[:, h1 * HEAD_DIM:(h1 + 1) * HEAD_DIM] = out[:, tq:].T


def _dsa_prompt(rel_bias, qi, sm, qa, kvab, bias_tiles, *, nb, s, tq, topk):
    nq = s // tq
    m = nb * s
    qrow = lambda b, j, rb: (b * nq + j, 0)
    in_specs = [
        pl.BlockSpec((tq, H_IDX * D_IDX), qrow),
        pl.BlockSpec((tq, LANES), qrow),
        pl.BlockSpec((s, LANES), lambda b, j, rb: (b, 0)),
        pl.BlockSpec((tq, W_A), qrow),
        pl.BlockSpec((s, 2 * KV_A), lambda b, j, rb: (b, 0)),
        pl.BlockSpec((2, H_A, tq, tq), lambda b, j, rb: (0, 0, 0, 0)),
    ]
    return pl.pallas_call(
        functools.partial(_dsa_prompt_kernel, topk=topk),
        grid_spec=pltpu.PrefetchScalarGridSpec(
            num_scalar_prefetch=1, grid=(nb, nq), in_specs=in_specs,
            out_specs=pl.BlockSpec((tq, W_A), qrow),
            scratch_shapes=[pltpu.VMEM((nq, tq, tq), I32),
                            pltpu.VMEM((nq, 2 * tq, LANES), BF16),
                            pltpu.VMEM((nq, HKV_A, HEAD_DIM, tq), BF16),
                            pltpu.VMEM((H_IDX // 2, LANES, tq), BF16),
                            pltpu.VMEM((H_A, HEAD_DIM, tq), BF16),
                            pltpu.VMEM((1, 2 * tq), F32),
                            pltpu.VMEM((1, 2 * tq), F32),
                            pltpu.VMEM((HEAD_DIM, 2 * tq), F32)]),
        out_shape=jax.ShapeDtypeStruct((m, W_A), F32),
        compiler_params=_cparams(2),
        name="dsa_prompt",
    )(rel_bias, qi, sm, sm, qa, kvab, bias_tiles)


def _sb_block(q, k, v, u_strict, mask, carry):
    z = lax.dot_general(q, k, (((1,), (1,)), ((), ())), preferred_element_type=F32) * SCALE
    l1p = jnp.log1p(jnp.exp(-jnp.abs(z)))
    ls_pos = jnp.minimum(z, 0.0) - l1p
    ls_neg = jnp.minimum(-z, 0.0) - l1p
    lb1 = ls_neg if mask is None else jnp.where(mask, ls_neg, 0.0)
    hi = lb1.astype(BF16)
    lo = (lb1 - hi.astype(F32)).astype(BF16)
    suffix = (jnp.dot(hi, u_strict, preferred_element_type=F32)
              + jnp.dot(lo, u_strict, preferred_element_type=F32)) + carry
    a = jnp.exp(ls_pos + suffix)
    if mask is not None:
        a = jnp.where(mask, a, 0.0)
    contrib = jnp.dot(a.astype(BF16), v, preferred_element_type=F32)
    return contrib, carry + jnp.sum(lb1, axis=1, keepdims=True)


def _sb_prompt_kernel(q_ref, k_ref, v_ref, u_ref, o_ref, acc_ref, carry_ref, *, heads):
    j = pl.program_id(2)
    tq = q_ref.shape[0]
    u = u_ref[...]
    row = lax.broadcasted_iota(I32, (tq, tq), 0)
    col = lax.broadcasted_iota(I32, (tq, tq), 1)
    hs = lambda h: slice(h * HEAD_DIM, (h + 1) * HEAD_DIM)

    off = pl.multiple_of(j * tq, tq)
    for h in range(heads):
        contrib, carry = _sb_block(q_ref[:, hs(h)], k_ref[pl.ds(off, tq), hs(h)], v_ref[pl.ds(off, tq), hs(h)],
                                   u, col < row, jnp.zeros((tq, 1), F32))
        acc_ref[h] = contrib
        carry_ref[h] = carry

    def cond(c):
        return (c >= 0) & (jnp.max(carry_ref[...]) > SB_DEAD)

    def body(c):
        off = pl.multiple_of(c * tq, tq)
        for h in range(heads):
            contrib, carry = _sb_block(q_ref[:, hs(h)], k_ref[pl.ds(off, tq), hs(h)],
                                       v_ref[pl.ds(off, tq), hs(h)], u, None, carry_ref[h])
            acc_ref[h] += contrib
            carry_ref[h] = carry
        return c - 1

    lax.while_loop(cond, body, j - 1)
    for h in range(heads):
        o_ref[:, hs(h)] = acc_ref[h]


def _sb_prompt(qb, kvbb, u_strict, *, nb, s, tq, heads):
    nq = s // tq
    m = nb * s
    w = heads * HEAD_DIM
    ng = H_B // heads
    return pl.pallas_call(
        functools.partial(_sb_prompt_kernel, heads=heads),
        grid=(nb, ng, nq),
        in_specs=[pl.BlockSpec((tq, w), lambda b, g, j: (b * nq + j, g)),
                  pl.BlockSpec((s, w), lambda b, g, j: (b, g)),
                  pl.BlockSpec((s, w), lambda b, g, j: (b, ng + g)),
                  pl.BlockSpec((tq, tq), lambda b, g, j: (0, 0))],
        out_specs=pl.BlockSpec((tq, w), lambda b, g, j: (b * nq + j, g)),
        out_shape=jax.ShapeDtypeStruct((m, W_B), F32),
        scratch_shapes=[pltpu.VMEM((heads, tq, HEAD_DIM), F32), pltpu.VMEM((heads, tq, 1), F32)],
        compiler_params=_cparams(3),
        name="sb_prompt",
    )(qb, kvbb, kvbb, u_strict)


def _merge_kernel(oa_ref, ob_ref, ga_ref, gb_ref, wo_ref, x_ref, gpost_ref, gate_ref,
                  gpre_ref, sh_ref, sc_ref, x1_ref, h2_ref):
    na = _rms(oa_ref[...], ga_ref[...]).astype(BF16)
    nb = _rms(ob_ref[...], gb_ref[...]).astype(BF16)
    o = (jnp.dot(na, wo_ref[:W_A, :], preferred_element_type=F32)
         + jnp.dot(nb, wo_ref[W_A:, :], preferred_element_type=F32))
    x1 = x_ref[...] + gate_ref[...] * _rms(o, gpost_ref[...])
    x1_ref[...] = x1
    h2_ref[...] = (_rms(x1, gpre_ref[...]) * (1.0 + sc_ref[...]) + sh_ref[...]).astype(BF16)


def _merge(oa, ob, g_a, g_b, w_out, x2d, g_post, gate, g_pre_ffn, shift, scale, *, tm, rows_per_group):
    m, d = x2d.shape
    bpg = rows_per_group // tm
    r = gate.shape[1]
    row = lambda i: (i, 0)
    grp = lambda i: (i // bpg, 0, 0)
    one = lambda i: (0, 0)
    return pl.pallas_call(
        _merge_kernel,
        grid=(m // tm,),
        in_specs=[pl.BlockSpec((tm, W_A), row), pl.BlockSpec((tm, W_B), row),
                  pl.BlockSpec((1, W_A), one), pl.BlockSpec((1, W_B), one),
                  pl.BlockSpec((W_A + W_B, d), one),
                  pl.BlockSpec((tm, d), row), pl.BlockSpec((1, d), one),
                  pl.BlockSpec((None, r, d), grp), pl.BlockSpec((1, d), one),
                  pl.BlockSpec((None, r, d), grp), pl.BlockSpec((None, r, d), grp)],
        out_specs=[pl.BlockSpec((tm, d), row), pl.BlockSpec((tm, d), row)],
        out_shape=[jax.ShapeDtypeStruct((m, d), F32), jax.ShapeDtypeStruct((m, d), BF16)],
        compiler_params=_cparams(1),
        name="merge_outproj",
    )(oa, ob, g_a, g_b, w_out, x2d, g_post, gate, g_pre_ffn, shift, scale)


def _ffn_kernel(h_ref, wg_ref, wu_ref, wd_ref, x1_ref, gpost_ref, gate_ref, y_ref, act_ref, f_ref, *, nf, nd):
    s = pl.program_id(1)

    @pl.when(s < nf)
    def _():
        h = h_ref[...]
        g = jnp.dot(h, wg_ref[...], preferred_element_type=F32)
        u = jnp.dot(h, wu_ref[...], preferred_element_type=F32)
        act_ref[s] = (g * jax.nn.sigmoid(g) * u).astype(BF16)

    @pl.when(s >= nf)
    def _():
        act = jnp.concatenate([act_ref[f] for f in range(nf)], axis=1)
        f_ref[s - nf] = jnp.dot(act, wd_ref[...], preferred_element_type=F32)

    @pl.when(s == nf + nd - 1)
    def _():
        fo = jnp.concatenate([f_ref[n] for n in range(nd)], axis=1)
        y_ref[...] = x1_ref[...] + gate_ref[...] * _rms(fo, gpost_ref[...])


def _ffn(h2, w_gu, w_down, x1, g_post, gate, *, tm, tf, td, rows_per_group):
    m, d = x1.shape
    ff = w_down.shape[0]
    nf = ff // tf
    nd = d // td
    bpg = rows_per_group // tm
    r = gate.shape[1]
    up_tile = lambda i, s: (0, jnp.minimum(s, nf - 1))
    return pl.pallas_call(
        functools.partial(_ffn_kernel, nf=nf, nd=nd),
        grid=(m // tm, nf + nd),
        in_specs=[pl.BlockSpec((tm, d), lambda i, s: (i, 0)),
                  pl.BlockSpec((d, tf), up_tile),
                  pl.BlockSpec((d, tf), lambda i, s: (0, nf + jnp.minimum(s, nf - 1))),
                  pl.BlockSpec((ff, td), lambda i, s: (0, jnp.maximum(s - nf, 0))),
                  pl.BlockSpec((tm, d), lambda i, s: (i, 0)),
                  pl.BlockSpec((1, d), lambda i, s: (0, 0)),
                  pl.BlockSpec((None, r, d), lambda i, s: (i // bpg, 0, 0))],
        out_specs=pl.BlockSpec((tm, d), lambda i, s: (i, 0)),
        out_shape=jax.ShapeDtypeStruct((m, d), F32),
        scratch_shapes=[pltpu.VMEM((nf, tm, tf), BF16), pltpu.VMEM((nd, tm, td), F32)],
        compiler_params=_cparams(2),
        name="swiglu_ffn",
    )(h2, w_gu, w_gu, w_down, x1, g_post, gate)


def _page_copy(src_ref, buf_ref, sem_ref, pt_ref, b, p, slot):
    return pltpu.make_async_copy(src_ref.at[pt_ref[b, p]], buf_ref.at[slot, p], sem_ref.at[slot])


def _sample_select_kernel(pt_ref, cidx_ref, q_ref, wsel_ref, knew_ref, lstrict_ref, ut_ref, eye_ref,
                          idx_ref, k3_ref, pbuf_ref, sem_ref, *, topk, t_new):
    b = pl.program_id(0)
    nb = pl.num_programs(0)
    n_pages = pbuf_ref.shape[1]
    n_rows = k3_ref.shape[1]
    slot = b % 2
    group = 8

    def start_all(bb, sl):
        def body(p, c):
            _page_copy(cidx_ref, pbuf_ref, sem_ref, pt_ref, bb, p, sl).start()
            return c
        lax.fori_loop(0, n_pages, body, 0)

    @pl.when(b == 0)
    def _():
        start_all(0, 0)

    @pl.when(b + 1 < nb)
    def _():
        start_all(b + 1, 1 - slot)

    def wait_body(p, c):
        _page_copy(cidx_ref, pbuf_ref, sem_ref, pt_ref, b, p, slot).wait()
        return c
    lax.fori_loop(0, n_pages, wait_body, 0)

    q = q_ref[...]
    wsel = wsel_ref[...]
    trow = lax.broadcasted_iota(I32, (8, LANES), 0)

    def to_keys(kt):
        dots = jnp.maximum(jnp.dot(q, kt, preferred_element_type=F32), 0.0)
        sc = jnp.dot(wsel, dots.astype(BF16), preferred_element_type=F32)
        return _float_key(sc)

    def score_group(gi, c):
        base = gi * group
        kt = jnp.concatenate([pbuf_ref[slot, base + u] for u in range(group)], axis=1).astype(BF16)
        keys = to_keys(kt)
        for t in range(t_new):
            for u in range(group):
                k3_ref[t, pl.ds(base + u, 1), :] = keys[t:t + 1, u * LANES:(u + 1) * LANES]
        return c
    lax.fori_loop(0, n_pages // group, score_group, 0)

    knew = to_keys(knew_ref[...])
    lane = lax.broadcasted_iota(I32, (8, LANES), 1)
    knew = jnp.where((lane <= trow) & (lane < t_new), knew, INT_MIN)
    pad = jnp.full((n_rows - n_pages - 1, LANES), INT_MIN, I32)
    for t in range(t_new):
        k3_ref[t, n_pages:n_pages + 1, :] = knew[t:t + 1]
        k3_ref[t, n_pages + 1:, :] = pad

    def search_cond(s):
        it = s[0]
        live = s[1] < s[1 + t_new]
        for t in range(1, t_new):
            live = live | (s[1 + t] < s[1 + t_new + t])
        return (it < 40) & live

    def search_body(s):
        it = s[0]
        los, his = [], []
        for t in range(t_new):
            lo, hi = s[1 + t], s[1 + t_new + t]
            mid = _ceil_avg(lo, hi)
            c = jnp.sum(jnp.where(k3_ref[t] >= mid, 1, 0))
            ge = c >= topk
            los.append(jnp.where(ge, mid, lo))
            his.append(jnp.where(c == topk, mid, jnp.where(ge, hi, mid - 1)))
        return (it + 1, *los, *his)

    init = (jnp.int32(0),) + (jnp.int32(INT_MIN + 1),) * t_new + (jnp.int32(INT_MAX),) * t_new
    res = lax.while_loop(search_cond, search_body, init)

    rowid = lax.broadcasted_iota(I32, (n_rows, 1), 0).astype(F32)
    qlane = lax.broadcasted_iota(I32, (1, 2 * LANES), 1).astype(F32)
    liota = lax.broadcasted_iota(I32, (LANES, 2 * LANES), 0).astype(F32)
    lstrict = lstrict_ref[...]
    idx_ref[...] = jnp.zeros(idx_ref.shape, I32)
    past = lax.broadcasted_iota(I32, (n_rows, 1), 0) < n_pages
    for t in range(t_new):
        thr = res[1 + t]
        sel = jnp.where((k3_ref[t] >= thr) & past, 1.0, 0.0)
        selb = sel.astype(BF16)
        cnt = jnp.sum(sel, axis=1, keepdims=True)
        start = jnp.dot(lstrict, jnp.broadcast_to(cnt, (n_rows, LANES)).astype(BF16),
                        preferred_element_type=F32)[:, :1]
        onehot = jnp.where((start <= qlane) & (qlane < start + cnt), 1.0, 0.0)
        nt = (((1,), (1,)), ((), ()))
        et = lax.dot_general(ut_ref[...], selb, nt, preferred_element_type=F32)
        selt = lax.dot_general(eye_ref[...], selb, nt, preferred_element_type=F32)
        ept = jnp.where(selt > 0.5, et, -1.0).astype(BF16)
        gt = jnp.dot(ept, onehot.astype(BF16), preferred_element_type=F32)
        start_q = jnp.sum(onehot * start, axis=0, keepdims=True)
        chunk_q = jnp.sum(onehot * rowid, axis=0, keepdims=True)
        lane_q = jnp.sum(jnp.where(gt == qlane - start_q, liota, 0.0), axis=0, keepdims=True)
        total = jnp.sum(cnt)
        ids = (chunk_q * LANES + lane_q).astype(I32)
        idx_ref[t:t + 1, :] = jnp.where(qlane < total, ids, -1)
        new_sel = jnp.where(k3_ref[t, n_pages:n_pages + 1, :] >= thr, 1, 0)
        idx_ref[4 + t:5 + t, :] = jnp.concatenate([new_sel, jnp.zeros_like(new_sel)], axis=1)


def _sample_select(page_table, cidx_t, q_s, wsel, knew_t, *, topk, t_new):
    db, n_pages = page_table.shape
    n_rows = ((n_pages + 1 + LANES - 1) // LANES) * LANES
    lstrict = jnp.asarray(np.tril(np.ones((n_rows, n_rows), np.float32), -1), BF16)
    ut = jnp.asarray(np.tril(np.ones((LANES, LANES), np.float32), -1), BF16)
    eye = jnp.asarray(np.eye(LANES, dtype=np.float32), BF16)
    const = lambda shape: pl.BlockSpec(shape, lambda b, pt: (0,) * len(shape))
    return pl.pallas_call(
        functools.partial(_sample_select_kernel, topk=topk, t_new=t_new),
        grid_spec=pltpu.PrefetchScalarGridSpec(
            num_scalar_prefetch=1, grid=(db,),
            in_specs=[pl.BlockSpec(memory_space=pl.ANY),
                      pl.BlockSpec((None,) + q_s.shape[1:], lambda b, pt: (b, 0, 0)),
                      pl.BlockSpec((None,) + wsel.shape[1:], lambda b, pt: (b, 0, 0)),
                      pl.BlockSpec((None,) + knew_t.shape[1:], lambda b, pt: (b, 0, 0)),
                      const((n_rows, n_rows)), const((LANES, LANES)), const((LANES, LANES))],
            out_specs=pl.BlockSpec((None, 8, 2 * LANES), lambda b, pt: (b, 0, 0)),
            scratch_shapes=[pltpu.VMEM((t_new, n_rows, LANES), I32),
                            pltpu.VMEM((2, n_pages, D_IDX, PAGE), F32),
                            pltpu.SemaphoreType.DMA((2,))]),
        out_shape=jax.ShapeDtypeStruct((db, 8, 2 * LANES), I32),
        compiler_params=_cparams(1),
        name="sample_select",
    )(page_table, cidx_t, q_s, wsel, knew_t, lstrict, ut, eye)


def _sample_attend_kernel(idx_s_ref, pt_ref, rb_ref, kva_ref, kvnew_ref, idxv_ref, q_ref, o_ref,
                          gbuf_ref, nbuf_ref, sem_ref, *, t_new, n_past):
    b = pl.program_id(0)
    nsel = idxv_ref.shape[1]
    rows = 2 * HKV_A
    unroll = 8

    new_rows = t_new * rows
    cp_new = pltpu.make_async_copy(kvnew_ref.at[pl.ds(b * new_rows, new_rows)],
                                   nbuf_ref.at[pl.ds(0, new_rows)], sem_ref.at[1])
    cp_new.start()
    nbuf_ref[pl.ds(new_rows, nbuf_ref.shape[0] - new_rows), :] = jnp.zeros(
        (nbuf_ref.shape[0] - new_rows, HEAD_DIM), F32)

    def row_copy(t, qi):
        i = jnp.maximum(idx_s_ref[b * t_new + t, qi], 0)
        page = lax.shift_right_logical(i, PAGE.bit_length() - 1)
        src = (pt_ref[b, page] * PAGE + (i & (PAGE - 1))) * rows
        return pltpu.make_async_copy(kva_ref.at[pl.ds(pl.multiple_of(src, rows), rows)],
                                     gbuf_ref.at[t, pl.ds(pl.multiple_of(qi * rows, rows), rows)],
                                     sem_ref.at[0])

    for t in range(t_new):
        def issue(qo, c):
            for u in range(unroll):
                row_copy(t, qo * unroll + u).start()
            return c
        lax.fori_loop(0, nsel // unroll, issue, 0)
    for t in range(t_new):
        def drain(qo, c):
            for u in range(unroll):
                row_copy(t, qo * unroll + u).wait()
            return c
        lax.fori_loop(0, nsel // unroll, drain, 0)
    cp_new.wait()

    nlane = lax.broadcasted_iota(I32, (1, LANES), 1)
    zpad = jnp.zeros((LANES - 8, HEAD_DIM), BF16)
    nt = (((1,), (1,)), ((), ()))
    for t in range(t_new):
        ids = idxv_ref[t:t + 1, :]
        valid = ids >= 0
        dist = (n_past + t) - ids
        new_valid = (idxv_ref[4 + t:5 + t, :LANES] > 0) & (nlane < t_new)
        new_dist = t - nlane
        for g in range(HKV_A):
            k = gbuf_ref[t, pl.ds(g, nsel, stride=rows), :].astype(BF16)
            v = gbuf_ref[t, pl.ds(HKV_A + g, nsel, stride=rows), :].astype(BF16)
            kn = jnp.concatenate([nbuf_ref[pl.ds(g, 8, stride=rows), :].astype(BF16), zpad], axis=0)
            vn = jnp.concatenate([nbuf_ref[pl.ds(HKV_A + g, 8, stride=rows), :].astype(BF16), zpad], axis=0)
            qg = q_ref[t, g]
            srow = lax.broadcasted_iota(I32, (8, 1), 0)
            s = lax.dot_general(qg, k, nt, preferred_element_type=F32) * SCALE
            s = s + jnp.where(srow == 0, _rel_bias_of(dist, rb_ref, 2 * g), _rel_bias_of(dist, rb_ref, 2 * g + 1))
            s = jnp.where(valid, s, -jnp.inf)
            sn = lax.dot_general(qg, kn, nt, preferred_element_type=F32) * SCALE
            sn = sn + jnp.where(srow == 0, _rel_bias_of(new_dist, rb_ref, 2 * g),
                                _rel_bias_of(new_dist, rb_ref, 2 * g + 1))
            sn = jnp.where(new_valid, sn, -jnp.inf)
            mx = jnp.maximum(jnp.max(s, axis=1, keepdims=True), jnp.max(sn, axis=1, keepdims=True))
            p = jnp.exp(s - mx)
            pn = jnp.exp(sn - mx)
            den = jnp.sum(p, axis=1, keepdims=True) + jnp.sum(pn, axis=1, keepdims=True)
            p = p / den
            pn = pn / den
            out = (jnp.dot(p.astype(BF16), v, preferred_element_type=F32)
                   + jnp.dot(pn.astype(BF16), vn, preferred_element_type=F32))
            o_ref[t:t + 1, (2 * g) * HEAD_DIM:(2 * g + 1) * HEAD_DIM] = out[0:1]
            o_ref[t:t + 1, (2 * g + 1) * HEAD_DIM:(2 * g + 2) * HEAD_DIM] = out[1:2]
    o_ref[t_new:, :] = jnp.zeros((8 - t_new, W_A), F32)


def _sample_attend(idx, page_table, rel_bias, kva_rows, kvnew_rows, q_pad, *, t_new, n_past):
    db = page_table.shape[0]
    nsel = idx.shape[-1]
    idx_s = idx[:, :t_new].reshape(db * t_new, nsel)
    return pl.pallas_call(
        functools.partial(_sample_attend_kernel, t_new=t_new, n_past=n_past),
        grid_spec=pltpu.PrefetchScalarGridSpec(
            num_scalar_prefetch=3, grid=(db,),
            in_specs=[pl.BlockSpec(memory_space=pl.ANY), pl.BlockSpec(memory_space=pl.ANY),
                      pl.BlockSpec((None, 8, nsel), lambda b, *_: (b, 0, 0)),
                      pl.BlockSpec((None,) + q_pad.shape[1:], lambda b, *_: (b, 0, 0, 0, 0))],
            out_specs=pl.BlockSpec((None, 8, W_A), lambda b, *_: (b, 0, 0)),
            scratch_shapes=[pltpu.VMEM((t_new, nsel * 2 * HKV_A, HEAD_DIM), F32),
                            pltpu.VMEM((8 * 2 * HKV_A, HEAD_DIM), F32),
                            pltpu.SemaphoreType.DMA((2,))]),
        out_shape=jax.ShapeDtypeStruct((db, 8, W_A), F32),
        compiler_params=_cparams(1),
        name="sample_attend",
    )(idx_s, page_table, rel_bias, kva_rows, kvnew_rows, idx, q_pad)


def _sb_sample_kernel(pt_ref, kvb_ref, kvnew_ref, q_ref, u_ref, o_ref, pbuf_ref, acc_ref, carry_ref,
                      sem_ref, *, t_new):
    b = pl.program_id(0)
    n_pages = pt_ref.shape[1]
    prow = pbuf_ref.shape[1]
    step = 2 * H_B
    u = u_ref[...]
    trow = lax.broadcasted_iota(I32, (8, PAGE), 0)
    scol = lax.broadcasted_iota(I32, (8, PAGE), 1)
    live_rows = lax.broadcasted_iota(I32, (8, 1), 0) < t_new

    def page_copy(p, slot):
        return pltpu.make_async_copy(kvb_ref.at[pl.ds(pt_ref[b, p] * prow, prow)], pbuf_ref.at[slot],
                                     sem_ref.at[slot])

    def process(slot, mask):
        for h in range(H_B):
            k = pbuf_ref[slot, pl.ds(h, PAGE, stride=step), :].astype(BF16)
            v = pbuf_ref[slot, pl.ds(H_B + h, PAGE, stride=step), :].astype(BF16)
            contrib, carry = _sb_block(q_ref[h], k, v, u, mask, carry_ref[h])
            acc_ref[h] += contrib
            carry_ref[h] = carry

    page_copy(n_pages - 1, 0).start()
    new_rows = t_new * step
    cp_new = pltpu.make_async_copy(kvnew_ref.at[pl.ds(b * new_rows, new_rows)],
                                   pbuf_ref.at[1, pl.ds(0, new_rows)], sem_ref.at[1])
    cp_new.start()
    pbuf_ref[1, pl.ds(new_rows, prow - new_rows), :] = jnp.zeros((prow - new_rows, HEAD_DIM), F32)
    acc_ref[...] = jnp.zeros(acc_ref.shape, F32)
    carry_ref[...] = jnp.zeros(carry_ref.shape, F32)
    cp_new.wait()
    process(1, (scol < trow) & (scol < t_new))

    def alive():
        c = jnp.where(live_rows, carry_ref[...], -jnp.inf)
        return jnp.max(c) > SB_DEAD

    def cond(s):
        i, go = s
        return (i < n_pages) & go

    def body(s):
        i, _ = s
        slot = i % 2
        page_copy(n_pages - 1 - i, slot).wait()

        @pl.when(i + 1 < n_pages)
        def _():
            page_copy(n_pages - 2 - i, 1 - slot).start()
        process(slot, None)
        return i + 1, alive()

    i_end, _ = lax.while_loop(cond, body, (jnp.int32(0), alive()))

    @pl.when(i_end < n_pages)
    def _():
        page_copy(n_pages - 1 - i_end, i_end % 2).wait()

    for h in range(H_B):
        o_ref[:, h * HEAD_DIM:(h + 1) * HEAD_DIM] = acc_ref[h]


def _sb_sample(page_table, kvb_rows, kvnew_rows, q_pad, u_strict, *, t_new):
    db = page_table.shape[0]
    prow = PAGE * 2 * H_B
    return pl.pallas_call(
        functools.partial(_sb_sample_kernel, t_new=t_new),
        grid_spec=pltpu.PrefetchScalarGridSpec(
            num_scalar_prefetch=1, grid=(db,),
            in_specs=[pl.BlockSpec(memory_space=pl.ANY), pl.BlockSpec(memory_space=pl.ANY),
                      pl.BlockSpec((None, H_B, 8, HEAD_DIM), lambda b, pt: (b, 0, 0, 0)),
                      pl.BlockSpec((PAGE, PAGE), lambda b, pt: (0, 0))],
            out_specs=pl.BlockSpec((None, 8, W_B), lambda b, pt: (b, 0, 0)),
            scratch_shapes=[pltpu.VMEM((2, prow, HEAD_DIM), F32),
                            pltpu.VMEM((H_B, 8, HEAD_DIM), F32),
                            pltpu.VMEM((H_B, 8, 1), F32),
                            pltpu.SemaphoreType.DMA((2,))]),
        out_shape=jax.ShapeDtypeStruct((db, 8, W_B), F32),
        compiler_params=_cparams(1),
        name="sb_sample",
    )(page_table, kvb_rows, kvnew_rows, q_pad, u_strict)


def _strict_upper(n):
    return jnp.asarray(np.tril(np.ones((n, n), np.float32), -1), BF16)


def kernel(x_prompt, x_sample, cache_kv_a, cache_idx, cache_kv_b, page_table, c_prompt, c_sample, rel_bias, w_ada, b_ada, g_pre_mix, w_in, g_grp_a, g_grp_b, w_out, g_post_mix, g_pre_ffn, w_gate_up, w_down, g_post_ffn):
    nb, s, d = x_prompt.shape
    db, t_new, _ = x_sample.shape
    n_pages = page_table.shape[1]
    n_past = n_pages * PAGE
    assert w_ada.shape[0] == 1 and d == (H_A + H_B) * HEAD_DIM
    assert cache_kv_a.shape[2:] == (PAGE, 2, HKV_A, HEAD_DIM) and t_new <= 4

    sp = np.cumsum([0, W_A, KV_A, KV_A, H_IDX * D_IDX, D_IDX, H_IDX, W_B, W_B, W_B])
    wi = w_in[0]
    col = lambda i: wi[:, sp[i]:sp[i + 1]]
    w_main = jnp.concatenate([col(0), col(1), col(2), col(3), col(6), col(7), col(8)], axis=1).astype(BF16)
    w_small = jnp.concatenate([col(4), col(5), jnp.zeros((d, LANES - D_IDX - H_IDX), F32)], axis=1).astype(BF16)
    w_small_t = w_small.T
    w_out_b = w_out[0].astype(BF16)
    w_gu_b = w_gate_up[0].astype(BF16)
    w_down_b = w_down[0].astype(BF16)

    mod = _modulation(jnp.concatenate([c_prompt, c_sample], axis=0), w_ada[0], b_ada)
    mod_p = mod[:nb].reshape(nb, 6, 1, d)
    mod_s = jnp.repeat(mod[nb:], t_new, axis=0).reshape(1, db * t_new, 6, d)
    mp = [mod_p[:, i] for i in range(6)]
    ms = [mod_s[:, :, i] for i in range(6)]

    tq = 256
    topk_p = min(TOPK_MAX, s // 4)
    xp = x_prompt.reshape(nb * s, d)
    (qa, kva, kvab, qi, qb, kvb, kvbb, sm, kt) = _project(
        xp, g_pre_mix, mp[0], mp[1], w_main, w_small, w_small_t, tm=min(512, s), rows_per_group=s)
    bias_tiles = _bias_tiles(rel_bias, tq)
    oa = _dsa_prompt(rel_bias, qi, sm, qa, kvab, bias_tiles, nb=nb, s=s, tq=tq, topk=topk_p)
    ob = _sb_prompt(qb, kvbb, _strict_upper(tq), nb=nb, s=s, tq=tq, heads=2)
    x1, h2 = _merge(oa, ob, g_grp_a, g_grp_b, w_out_b, xp, g_post_mix, mp[2], g_pre_ffn, mp[3], mp[4],
                    tm=256, rows_per_group=s)
    y_prompt = _ffn(h2, w_gu_b, w_down_b, x1, g_post_ffn, mp[5], tm=512, tf=512, td=512, rows_per_group=s)

    ms_rows = db * t_new
    topk_s = min(TOPK_MAX, (n_past + t_new) // 4)
    xs = x_sample.reshape(ms_rows, d)
    (qa_s, kva_s, _, qi_s, qb_s, kvb_s, _, sm_s, kt_s) = _project(
        xs, g_pre_mix, ms[0], ms[1], w_main, w_small, w_small_t, tm=ms_rows, rows_per_group=ms_rows)

    q_s = qi_s.reshape(db, t_new * H_IDX, D_IDX)
    w_s = sm_s[:, D_IDX:D_IDX + H_IDX].reshape(db, t_new, H_IDX)
    wsel = (jnp.eye(8, t_new, dtype=F32)[None, :, :, None] * w_s[:, None]).reshape(db, 8, t_new * H_IDX)
    knew_t = jnp.pad(kt_s[0].reshape(D_IDX, db, t_new).transpose(1, 0, 2),
                     ((0, 0), (0, 0), (0, LANES - t_new))).astype(BF16)
    cidx_t = jnp.swapaxes(cache_idx[0], 1, 2)
    idx = _sample_select(page_table, cidx_t, q_s, wsel.astype(BF16), knew_t, topk=topk_s, t_new=t_new)

    kva_rows = cache_kv_a.reshape(-1, HEAD_DIM)
    kvnew_rows = kva_s.reshape(-1, HEAD_DIM)
    qa4 = qa_s.reshape(db, t_new, HKV_A, H_A // HKV_A, HEAD_DIM)
    q_pad = jnp.pad(qa4, ((0, 0), (0, 0), (0, 0), (0, 8 - H_A // HKV_A), (0, 0)))
    oa_s = _sample_attend(idx, page_table, rel_bias, kva_rows, kvnew_rows, q_pad, t_new=t_new, n_past=n_past)

    kvb_rows = cache_kv_b.reshape(-1, HEAD_DIM)
    kvbnew_rows = kvb_s.reshape(-1, HEAD_DIM)
    qb_pad = jnp.pad(qb_s.reshape(db, t_new, H_B, HEAD_DIM).transpose(0, 2, 1, 3),
                     ((0, 0), (0, 0), (0, 8 - t_new), (0, 0)))
    ob_s = _sb_sample(page_table, kvb_rows, kvbnew_rows, qb_pad, _strict_upper(PAGE), t_new=t_new)

    oa_s = oa_s[:, :t_new].reshape(ms_rows, W_A)
    ob_s = ob_s[:, :t_new].reshape(ms_rows, W_B)
    x1_s, h2_s = _merge(oa_s, ob_s, g_grp_a, g_grp_b, w_out_b, xs, g_post_mix, ms[2], g_pre_ffn, ms[3],
                        ms[4], tm=ms_rows, rows_per_group=ms_rows)
    y_sample = _ffn(h2_s, w_gu_b, w_down_b, x1_s, g_post_ffn, ms[5], tm=ms_rows, tf=512, td=512,
                    rows_per_group=ms_rows)

    return (y_prompt.reshape(nb, s, d),
            y_sample.reshape(db, t_new, d),
            kva.reshape(1, nb, s, 2, HKV_A, HEAD_DIM),
            jnp.swapaxes(kt, 1, 2)[None],
            kvb.reshape(1, nb, s, 2, H_B, HEAD_DIM),
            kva_s.reshape(1, db, t_new, 2, HKV_A, HEAD_DIM),
            sm_s[:, :D_IDX].reshape(1, db, t_new, D_IDX),
            kvb_s.reshape(1, db, t_new, 2, H_B, HEAD_DIM))
```

```python
import functools
import math

import jax
import jax.numpy as jnp
import numpy as np
from jax import lax
from jax.experimental import pallas as pl
from jax.experimental.pallas import tpu as pltpu

F32 = jnp.float32
BF16 = jnp.bfloat16
I32 = jnp.int32

HEAD_DIM = 128
H_A = 8
HKV_A = 4
H_B = 8
H_IDX = 16
D_IDX = 64
TOPK_MAX = 256
N_BUCKETS = 32
MAX_DISTANCE = 128
PAGE = 128
EPS = 1e-6
SCALE = HEAD_DIM ** -0.5
IDX_W_SCALE = (H_IDX * D_IDX) ** -0.5
W_A = H_A * HEAD_DIM
KV_A = HKV_A * HEAD_DIM
W_B = H_B * HEAD_DIM
INT_MIN = -2 ** 31
INT_MAX = 2 ** 31 - 1
SB_DEAD = -120.0

LANES = 128
VMEM_LIMIT = 56 * 1024 * 1024


def _cparams(n_axes, vmem=VMEM_LIMIT):
    return pltpu.CompilerParams(dimension_semantics=("arbitrary",) * n_axes,
                                vmem_limit_bytes=vmem)


def _bucket_thresholds():
    max_exact = N_BUCKETS // 2
    d = np.arange(0, 4 * MAX_DISTANCE, dtype=np.int32)
    df = np.maximum(d, 1).astype(np.float32)
    large = max_exact + (np.log(df / np.float32(max_exact)) / np.float32(math.log(MAX_DISTANCE / max_exact))
                         * np.float32(N_BUCKETS - max_exact)).astype(np.int32)
    large = np.minimum(large, N_BUCKETS - 1)
    bucket = np.where(d < max_exact, d, large)
    assert bucket[-1] == N_BUCKETS - 1 and np.all(np.diff(bucket) >= 0)
    return [int(np.argmax(bucket >= b)) for b in range(1, N_BUCKETS)]


_BUCKET_THR = _bucket_thresholds()


def _rel_bias_of(dist, rb_ref, h):
    val = jnp.full(dist.shape, rb_ref[0, h], F32)
    for b in range(1, N_BUCKETS):
        val = jnp.where(dist >= _BUCKET_THR[b - 1], rb_ref[b, h], val)
    return val


def _float_key(x):
    bits = lax.bitcast_convert_type(x, I32)
    return bits ^ ((bits >> 31) & INT_MAX)


def _ceil_avg(lo, hi):
    x = lo ^ hi
    return (lo & hi) + (x >> 1) + (x & 1)


def _rms(x, g):
    return x * lax.rsqrt(jnp.mean(x * x, axis=-1, keepdims=True) + EPS) * g


def _mod_kernel(c_ref, w_ref, b_ref, o_ref):
    c = c_ref[...]
    a = c * jax.nn.sigmoid(c)
    o_ref[...] = jnp.dot(a, w_ref[...], preferred_element_type=F32,
                         precision=lax.Precision.HIGHEST) + b_ref[...]


def _modulation(c_all, w_ada, b_ada):
    r, d = c_all.shape
    n = w_ada.shape[1]
    tn = 1024
    return pl.pallas_call(
        _mod_kernel,
        grid=(n // tn,),
        in_specs=[pl.BlockSpec((r, d), lambda i: (0, 0)),
                  pl.BlockSpec((d, tn), lambda i: (0, i)),
                  pl.BlockSpec((1, tn), lambda i: (0, i))],
        out_specs=pl.BlockSpec((r, tn), lambda i: (0, i)),
        out_shape=jax.ShapeDtypeStruct((r, n), F32),
        compiler_params=_cparams(1),
        name="adaln_mod",
    )(c_all, w_ada, b_ada)


PROJ_TN = 1024
_PROJ_GROUPS = {"qa": (0, 1), "kva": (1, 1), "qi": (2, 1), "qb": (3, 1), "kvb": (4, 2)}
N_MAIN = 6


def _proj_kernel(x_ref, g_ref, sh_ref, sc_ref, wm_ref, ws_ref, wst_ref,
                 qa_ref, kva_ref, kvab_ref, qi_ref, qb_ref, kvb_ref, kvbb_ref,
                 sm_ref, kt_ref, h_ref):
    n = pl.program_id(1)

    @pl.when(n == 0)
    def _():
        y = _rms(x_ref[...], g_ref[...])
        h_ref[...] = (y * (1.0 + sc_ref[...]) + sh_ref[...]).astype(BF16)

    def mm():
        return jnp.dot(h_ref[...], wm_ref[...], preferred_element_type=F32)

    def in_group(name):
        first, count = _PROJ_GROUPS[name]
        return (n >= first) & (n < first + count)

    @pl.when(in_group("qa"))
    def _():
        qa_ref[...] = mm().astype(BF16)

    def store_heads(out_ref, r, first_row, rows_per_token):
        for j in range(r.shape[1] // HEAD_DIM):
            out_ref[pl.ds(first_row + j, r.shape[0], stride=rows_per_token), :] = r[:, j * HEAD_DIM:(j + 1) * HEAD_DIM]

    @pl.when(in_group("kva"))
    def _():
        r = mm()
        store_heads(kva_ref, r, 0, 2 * HKV_A)
        kvab_ref[...] = r.astype(BF16)

    @pl.when(in_group("qi"))
    def _():
        qi_ref[...] = mm().astype(BF16)

    @pl.when(in_group("qb"))
    def _():
        qb_ref[...] = mm().astype(BF16)

    for part in range(_PROJ_GROUPS["kvb"][1]):
        @pl.when(n == _PROJ_GROUPS["kvb"][0] + part)
        def _():
            r = mm()
            store_heads(kvb_ref, r, part * H_B, 2 * H_B)
            kvbb_ref[...] = r.astype(BF16)

    @pl.when(n == N_MAIN)
    def _():
        r = jnp.dot(h_ref[...], ws_ref[...], preferred_element_type=F32)
        lane = lax.broadcasted_iota(I32, r.shape, 1)
        is_w = (lane >= D_IDX) & (lane < D_IDX + H_IDX)
        sm_ref[...] = jnp.where(is_w, r * IDX_W_SCALE, r)
        rt = lax.dot_general(wst_ref[...], h_ref[...], (((1,), (1,)), ((), ())),
                             preferred_element_type=F32)
        kt_ref[...] = rt[:D_IDX]


def _project(x2d, g_pre, shift, scale, w_main, w_small, w_small_t, *, tm, rows_per_group):
    m, d = x2d.shape
    nb = m // tm
    bpg = rows_per_group // tm
    r = shift.shape[1]
    ng = m // rows_per_group
    tn = PROJ_TN
    grid = (nb, N_MAIN + 1)
    row = lambda i, n: (i, 0)
    grp = lambda i, n: (i // bpg, 0, 0)

    def tile(name):
        first, count = _PROJ_GROUPS[name]
        return pl.BlockSpec((tm, tn), lambda i, n: (i, jnp.clip(n - first, 0, count - 1)))

    out_shape = [
        jax.ShapeDtypeStruct((m, W_A), BF16),
        jax.ShapeDtypeStruct((m * 2 * HKV_A, HEAD_DIM), F32),
        jax.ShapeDtypeStruct((m, 2 * KV_A), BF16),
        jax.ShapeDtypeStruct((m, H_IDX * D_IDX), BF16),
        jax.ShapeDtypeStruct((m, W_B), BF16),
        jax.ShapeDtypeStruct((m * 2 * H_B, HEAD_DIM), F32),
        jax.ShapeDtypeStruct((m, 2 * W_B), BF16),
        jax.ShapeDtypeStruct((m, LANES), F32),
        jax.ShapeDtypeStruct((ng, D_IDX, rows_per_group), F32),
    ]
    out_specs = [
        tile("qa"), pl.BlockSpec((tm * 2 * HKV_A, HEAD_DIM), row), tile("kva"), tile("qi"), tile("qb"),
        pl.BlockSpec((tm * 2 * H_B, HEAD_DIM), row), tile("kvb"),
        pl.BlockSpec((tm, LANES), row),
        pl.BlockSpec((None, D_IDX, tm), lambda i, n: (i // bpg, 0, i % bpg)),
    ]
    in_specs = [
        pl.BlockSpec((tm, d), row),
        pl.BlockSpec((1, d), lambda i, n: (0, 0)),
        pl.BlockSpec((None, r, d), grp),
        pl.BlockSpec((None, r, d), grp),
        pl.BlockSpec((d, tn), lambda i, n: (0, jnp.minimum(n, N_MAIN - 1))),
        pl.BlockSpec((d, LANES), lambda i, n: (0, 0)),
        pl.BlockSpec((LANES, d), lambda i, n: (0, 0)),
    ]
    return pl.pallas_call(
        _proj_kernel,
        grid=grid, in_specs=in_specs, out_specs=out_specs, out_shape=out_shape,
        scratch_shapes=[pltpu.VMEM((tm, d), BF16)],
        compiler_params=_cparams(2),
        name="prenorm_proj",
    )(x2d, g_pre, shift, scale, w_main, w_small, w_small_t)


def _bias_tiles_kernel(rb_ref, o_ref):
    h = pl.program_id(0)
    tq = o_ref.shape[-1]
    key = lax.broadcasted_iota(I32, (tq, tq), 0)
    qry = lax.broadcasted_iota(I32, (tq, tq), 1)
    for k in range(2):
        o_ref[k] = _rel_bias_of(qry - key + k * tq, rb_ref, h)


def _bias_tiles(rel_bias, tq):
    return pl.pallas_call(
        _bias_tiles_kernel,
        grid=(H_A,),
        in_specs=[pl.BlockSpec(memory_space=pltpu.SMEM)],
        out_specs=pl.BlockSpec((2, None, tq, tq), lambda h: (0, h, 0, 0)),
        out_shape=jax.ShapeDtypeStruct((2, H_A, tq, tq), F32),
        compiler_params=_cparams(1),
        name="t5_bias_tiles",
    )(rel_bias)


GROUPS_PER_PASS = 2


def _dsa_prompt_kernel(rb_ref, qi_ref, smq_ref, smk_ref, qa_ref, kv_ref, bias_ref,
                       o_ref, keys_ref, kab_ref, vt_ref, qit_ref, qat_ref, m_ref, l_ref, acc_ref, *, topk):
    j = pl.program_id(1)
    tq = qa_ref.shape[0]
    nq = keys_ref.shape[0]
    nkc = j + 1
    qpos = j * tq + lax.broadcasted_iota(I32, (1, tq), 1)

    @pl.when(j == 0)
    def _():
        def prep(c, carry):
            off = pl.multiple_of(c * tq, tq)
            r = smk_ref[pl.ds(off, tq), :]
            lane = lax.broadcasted_iota(I32, r.shape, 1)
            ka = jnp.where(lane < D_IDX, r, 0.0)
            kb = jnp.where(lane >= D_IDX, pltpu.roll(r, D_IDX, axis=1), 0.0)
            kab_ref[c] = jnp.concatenate([ka, kb], axis=0).astype(BF16)
            for g in range(HKV_A):
                v = kv_ref[pl.ds(off, tq), KV_A + g * HEAD_DIM:KV_A + (g + 1) * HEAD_DIM]
                vt_ref[c, g] = v.astype(F32).T.astype(BF16)
            return carry

        lax.fori_loop(0, nq, prep, 0)

    for hp in range(H_IDX // 2):
        qit_ref[hp] = qi_ref[:, hp * LANES:(hp + 1) * LANES].astype(F32).T.astype(BF16)
    for h in range(H_A):
        qat_ref[h] = qa_ref[:, h * HEAD_DIM:(h + 1) * HEAD_DIM].astype(F32).T.astype(BF16)
    wt = smq_ref[...].T[D_IDX:D_IDX + H_IDX]

    def score_chunk(c, carry):
        kab = kab_ref[c]
        acc = jnp.zeros((tq, tq), F32)
        for hp in range(H_IDX // 2):
            d = jnp.maximum(jnp.dot(kab, qit_ref[hp], preferred_element_type=F32), 0.0)
            acc = acc + wt[2 * hp:2 * hp + 1] * d[:tq] + wt[2 * hp + 1:2 * hp + 2] * d[tq:]
        kpos = c * tq + lax.broadcasted_iota(I32, (tq, 1), 0)
        keys_ref[c] = jnp.where(kpos <= qpos, _float_key(acc), INT_MIN)
        return carry

    lax.fori_loop(0, nkc, score_chunk, 0)

    need = (qpos + 1) > topk

    def count_ge(mid):
        def body(c, cnt):
            hit = jnp.where(keys_ref[c] >= mid, 1, 0)
            return cnt + jnp.sum(hit.reshape(tq // 8, 8, tq), axis=0)

        cnt = lax.fori_loop(0, nkc, body, jnp.zeros((8, tq), I32))
        return jnp.sum(cnt, axis=0, keepdims=True)

    def search_body(_, s):
        lo, hi = s
        mid = _ceil_avg(lo, hi)
        c = count_ge(mid)
        ge = c >= topk
        return jnp.where(ge, mid, lo), jnp.where(c == topk, mid, jnp.where(ge, hi, mid - 1))

    lo0 = jnp.full((1, tq), INT_MIN + 1, I32)
    hi0 = jnp.where(need, INT_MAX, INT_MIN + 1)
    thr, _ = lax.fori_loop(0, 32, search_body, (lo0, hi0))

    n_gt = count_ge(thr + 1)
    excess = count_ge(thr) > topk

    @pl.when(jnp.max(jnp.where(excess, 1.0, 0.0)) > 0.5)
    def _():
        keep = topk - n_gt

        def tied_upto(pos):
            def body(c, cnt):
                kpos = c * tq + lax.broadcasted_iota(I32, (tq, 1), 0)
                hit = jnp.where((keys_ref[c] == thr) & (kpos <= pos), 1, 0)
                return cnt + jnp.sum(hit.reshape(tq // 8, 8, tq), axis=0)

            cnt = lax.fori_loop(0, nkc, body, jnp.zeros((8, tq), I32))
            return jnp.sum(cnt, axis=0, keepdims=True)

        def pos_body(_, s):
            lo, hi = s
            mid = (lo + hi) >> 1
            ok = tied_upto(mid) >= keep
            return jnp.where(ok, lo, mid + 1), jnp.where(ok, mid, hi)

        last, _ = lax.fori_loop(0, (nq * tq - 1).bit_length(), pos_body,
                                (jnp.zeros((1, tq), I32), jnp.full((1, tq), nq * tq - 1, I32)))

        def demote(c, carry):
            kc = keys_ref[c]
            kpos = c * tq + lax.broadcasted_iota(I32, (tq, 1), 0)
            keys_ref[c] = jnp.where(excess & (kc == thr) & (kpos > last), thr - 1, kc)
            return carry

        lax.fori_loop(0, nkc, demote, 0)

    far = N_BUCKETS - 1
    left = lax.broadcasted_iota(I32, (1, 2 * tq), 1) < tq
    for g0 in range(0, HKV_A, GROUPS_PER_PASS):
        groups = range(g0, g0 + GROUPS_PER_PASS)
        qt2 = [jnp.concatenate([qat_ref[2 * g], qat_ref[2 * g + 1]], axis=1) for g in groups]
        m_ref[...] = jnp.full(m_ref.shape, -jnp.inf, F32)
        l_ref[...] = jnp.zeros(l_ref.shape, F32)
        acc_ref[...] = jnp.zeros(acc_ref.shape, F32)

        def block(c, bias_of):
            off = pl.multiple_of(c * tq, tq)
            sel = keys_ref[c] >= thr
            sel2 = jnp.concatenate([sel, sel], axis=1)
            for i, g in enumerate(groups):
                k = kv_ref[pl.ds(off, tq), g * HEAD_DIM:(g + 1) * HEAD_DIM]
                s = jnp.dot(k, qt2[i], preferred_element_type=F32) * SCALE + bias_of(g)
                s = jnp.where(sel2, s, -jnp.inf)
                m_old = m_ref[i]
                m_new = jnp.maximum(m_old, jnp.max(s, axis=0, keepdims=True))
                m_safe = jnp.where(m_new == -jnp.inf, 0.0, m_new)
                p = jnp.exp(s - m_safe)
                alpha = jnp.exp(m_old - m_safe)
                l_ref[i] = alpha * l_ref[i] + jnp.sum(p, axis=0, keepdims=True)
                acc_ref[i] = alpha * acc_ref[i] + jnp.dot(vt_ref[c, g], p.astype(BF16),
                                                          preferred_element_type=F32)
                m_ref[i] = m_new

        far_bias = {g: jnp.where(left, rb_ref[far, 2 * g], rb_ref[far, 2 * g + 1]) for g in groups}

        def far_block(c, carry):
            block(c, lambda g: far_bias[g])
            return carry

        lax.fori_loop(0, j - 1, far_block, 0)

        def near_bias(k):
            return lambda g: jnp.concatenate([bias_ref[k, 2 * g], bias_ref[k, 2 * g + 1]], axis=1)

        @pl.when(j >= 1)
        def _():
            block(j - 1, near_bias(1))

        block(j, near_bias(0))

        for i, g in enumerate(groups):
            out = acc_ref[i] / l_ref[i]
            o_ref[:, (2 * g) * HEAD_DIM:(2 * g + 1) * HEAD_DIM] = out[:, :tq].T
            o_ref[:, (2 * g + 1) * HEAD_DIM:(2 * g + 2) * HEAD_DIM] = out[:, tq:].T


def _dsa_prompt(rel_bias, qi, sm, qa, kvab, bias_tiles, *, nb, s, tq, topk):
    nq = s // tq
    m = nb * s
    qrow = lambda b, j, rb: (b * nq + j, 0)
    in_specs = [
        pl.BlockSpec((tq, H_IDX * D_IDX), qrow),
        pl.BlockSpec((tq, LANES), qrow),
        pl.BlockSpec((s, LANES), lambda b, j, rb: (b, 0)),
        pl.BlockSpec((tq, W_A), qrow),
        pl.BlockSpec((s, 2 * KV_A), lambda b, j, rb: (b, 0)),
        pl.BlockSpec((2, H_A, tq, tq), lambda b, j, rb: (0, 0, 0, 0)),
    ]
    return pl.pallas_call(
        functools.partial(_dsa_prompt_kernel, topk=topk),
        grid_spec=pltpu.PrefetchScalarGridSpec(
            num_scalar_prefetch=1, grid=(nb, nq), in_specs=in_specs,
            out_specs=pl.BlockSpec((tq, W_A), qrow),
            scratch_shapes=[pltpu.VMEM((nq, tq, tq), I32),
                            pltpu.VMEM((nq, 2 * tq, LANES), BF16),
                            pltpu.VMEM((nq, HKV_A, HEAD_DIM, tq), BF16),
                            pltpu.VMEM((H_IDX // 2, LANES, tq), BF16),
                            pltpu.VMEM((H_A, HEAD_DIM, tq), BF16),
                            pltpu.VMEM((GROUPS_PER_PASS, 1, 2 * tq), F32),
                            pltpu.VMEM((GROUPS_PER_PASS, 1, 2 * tq), F32),
                            pltpu.VMEM((GROUPS_PER_PASS, HEAD_DIM, 2 * tq), F32)]),
        out_shape=jax.ShapeDtypeStruct((m, W_A), F32),
        compiler_params=_cparams(2),
        name="dsa_prompt",
    )(rel_bias, qi, sm, sm, qa, kvab, bias_tiles)


def _sb_block(q, k, v, u_strict, mask, carry):
    z = lax.dot_general(q, k, (((1,), (1,)), ((), ())), preferred_element_type=F32) * SCALE
    l1p = jnp.log1p(jnp.exp(-jnp.abs(z)))
    ls_pos = jnp.minimum(z, 0.0) - l1p
    ls_neg = jnp.minimum(-z, 0.0) - l1p
    lb1 = ls_neg if mask is None else jnp.where(mask, ls_neg, 0.0)
    hi = lb1.astype(BF16)
    lo = (lb1 - hi.astype(F32)).astype(BF16)
    suffix = (jnp.dot(hi, u_strict, preferred_element_type=F32)
              + jnp.dot(lo, u_strict, preferred_element_type=F32)) + carry
    a = jnp.exp(ls_pos + suffix)
    if mask is not None:
        a = jnp.where(mask, a, 0.0)
    contrib = jnp.dot(a.astype(BF16), v, preferred_element_type=F32)
    return contrib, carry + jnp.sum(lb1, axis=1, keepdims=True)


def _sb_prompt_kernel(q_ref, k_ref, v_ref, u_ref, o_ref, acc_ref, carry_ref, *, heads):
    j = pl.program_id(2)
    tq = q_ref.shape[0]
    u = u_ref[...]
    row = lax.broadcasted_iota(I32, (tq, tq), 0)
    col = lax.broadcasted_iota(I32, (tq, tq), 1)
    hs = lambda h: slice(h * HEAD_DIM, (h + 1) * HEAD_DIM)

    off = pl.multiple_of(j * tq, tq)
    for h in range(heads):
        contrib, carry = _sb_block(q_ref[:, hs(h)], k_ref[pl.ds(off, tq), hs(h)], v_ref[pl.ds(off, tq), hs(h)],
                                   u, col < row, jnp.zeros((tq, 1), F32))
        acc_ref[h] = contrib
        carry_ref[h] = carry

    def cond(c):
        return (c >= 0) & (jnp.max(carry_ref[...]) > SB_DEAD)

    def body(c):
        off = pl.multiple_of(c * tq, tq)
        for h in range(heads):
            contrib, carry = _sb_block(q_ref[:, hs(h)], k_ref[pl.ds(off, tq), hs(h)],
                                       v_ref[pl.ds(off, tq), hs(h)], u, None, carry_ref[h])
            acc_ref[h] += contrib
            carry_ref[h] = carry
        return c - 1

    lax.while_loop(cond, body, j - 1)
    for h in range(heads):
        o_ref[:, hs(h)] = acc_ref[h]


def _sb_prompt(qb, kvbb, u_strict, *, nb, s, tq, heads):
    nq = s // tq
    m = nb * s
    w = heads * HEAD_DIM
    ng = H_B // heads
    return pl.pallas_call(
        functools.partial(_sb_prompt_kernel, heads=heads),
        grid=(nb, ng, nq),
        in_specs=[pl.BlockSpec((tq, w), lambda b, g, j: (b * nq + j, g)),
                  pl.BlockSpec((s, w), lambda b, g, j: (b, g)),
                  pl.BlockSpec((s, w), lambda b, g, j: (b, ng + g)),
                  pl.BlockSpec((tq, tq), lambda b, g, j: (0, 0))],
        out_specs=pl.BlockSpec((tq, w), lambda b, g, j: (b * nq + j, g)),
        out_shape=jax.ShapeDtypeStruct((m, W_B), F32),
        scratch_shapes=[pltpu.VMEM((heads, tq, HEAD_DIM), F32), pltpu.VMEM((heads, tq, 1), F32)],
        compiler_params=_cparams(3),
        name="sb_prompt",
    )(qb, kvbb, kvbb, u_strict)


def _merge_kernel(oa_ref, ob_ref, ga_ref, gb_ref, wo_ref, x_ref, gpost_ref, gate_ref,
                  gpre_ref, sh_ref, sc_ref, x1_ref, h2_ref):
    na = _rms(oa_ref[...], ga_ref[...]).astype(BF16)
    nb = _rms(ob_ref[...], gb_ref[...]).astype(BF16)
    o = (jnp.dot(na, wo_ref[:W_A, :], preferred_element_type=F32)
         + jnp.dot(nb, wo_ref[W_A:, :], preferred_element_type=F32))
    x1 = x_ref[...] + gate_ref[...] * _rms(o, gpost_ref[...])
    x1_ref[...] = x1
    h2_ref[...] = (_rms(x1, gpre_ref[...]) * (1.0 + sc_ref[...]) + sh_ref[...]).astype(BF16)


def _merge(oa, ob, g_a, g_b, w_out, x2d, g_post, gate, g_pre_ffn, shift, scale, *, tm, rows_per_group):
    m, d = x2d.shape
    bpg = rows_per_group // tm
    r = gate.shape[1]
    row = lambda i: (i, 0)
    grp = lambda i: (i // bpg, 0, 0)
    one = lambda i: (0, 0)
    return pl.pallas_call(
        _merge_kernel,
        grid=(m // tm,),
        in_specs=[pl.BlockSpec((tm, W_A), row), pl.BlockSpec((tm, W_B), row),
                  pl.BlockSpec((1, W_A), one), pl.BlockSpec((1, W_B), one),
                  pl.BlockSpec((W_A + W_B, d), one),
                  pl.BlockSpec((tm, d), row), pl.BlockSpec((1, d), one),
                  pl.BlockSpec((None, r, d), grp), pl.BlockSpec((1, d), one),
                  pl.BlockSpec((None, r, d), grp), pl.BlockSpec((None, r, d), grp)],
        out_specs=[pl.BlockSpec((tm, d), row), pl.BlockSpec((tm, d), row)],
        out_shape=[jax.ShapeDtypeStruct((m, d), F32), jax.ShapeDtypeStruct((m, d), BF16)],
        compiler_params=_cparams(1),
        name="merge_outproj",
    )(oa, ob, g_a, g_b, w_out, x2d, g_post, gate, g_pre_ffn, shift, scale)


def _ffn_kernel(h_ref, wg_ref, wu_ref, wd_ref, x1_ref, gpost_ref, gate_ref, y_ref, act_ref, f_ref, *, nf, nd):
    s = pl.program_id(1)

    @pl.when(s < nf)
    def _():
        h = h_ref[...]
        g = jnp.dot(h, wg_ref[...], preferred_element_type=F32)
        u = jnp.dot(h, wu_ref[...], preferred_element_type=F32)
        act_ref[s] = (g * jax.nn.sigmoid(g) * u).astype(BF16)

    @pl.when(s >= nf)
    def _():
        act = jnp.concatenate([act_ref[f] for f in range(nf)], axis=1)
        f_ref[s - nf] = jnp.dot(act, wd_ref[...], preferred_element_type=F32)

    @pl.when(s == nf + nd - 1)
    def _():
        fo = jnp.concatenate([f_ref[n] for n in range(nd)], axis=1)
        y_ref[...] = x1_ref[...] + gate_ref[...] * _rms(fo, gpost_ref[...])


def _ffn(h2, w_gu, w_down, x1, g_post, gate, *, tm, tf, td, rows_per_group):
    m, d = x1.shape
    ff = w_down.shape[0]
    nf = ff // tf
    nd = d // td
    bpg = rows_per_group // tm
    r = gate.shape[1]
    up_tile = lambda i, s: (0, jnp.minimum(s, nf - 1))
    return pl.pallas_call(
        functools.partial(_ffn_kernel, nf=nf, nd=nd),
        grid=(m // tm, nf + nd),
        in_specs=[pl.BlockSpec((tm, d), lambda i, s: (i, 0)),
                  pl.BlockSpec((d, tf), up_tile),
                  pl.BlockSpec((d, tf), lambda i, s: (0, nf + jnp.minimum(s, nf - 1))),
                  pl.BlockSpec((ff, td), lambda i, s: (0, jnp.maximum(s - nf, 0))),
                  pl.BlockSpec((tm, d), lambda i, s: (i, 0)),
                  pl.BlockSpec((1, d), lambda i, s: (0, 0)),
                  pl.BlockSpec((None, r, d), lambda i, s: (i // bpg, 0, 0))],
        out_specs=pl.BlockSpec((tm, d), lambda i, s: (i, 0)),
        out_shape=jax.ShapeDtypeStruct((m, d), F32),
        scratch_shapes=[pltpu.VMEM((nf, tm, tf), BF16), pltpu.VMEM((nd, tm, td), F32)],
        compiler_params=_cparams(2),
        name="swiglu_ffn",
    )(h2, w_gu, w_gu, w_down, x1, g_post, gate)


def _page_copy(src_ref, buf_ref, sem_ref, pt_ref, b, p, slot):
    return pltpu.make_async_copy(src_ref.at[pt_ref[b, p]], buf_ref.at[slot, p], sem_ref.at[slot])


def _sample_select_kernel(pt_ref, cidx_ref, q_ref, wsel_ref, knew_ref, lstrict_ref, ut_ref, eye_ref,
                          idx_ref, k3_ref, pbuf_ref, sem_ref, *, topk, t_new):
    b = pl.program_id(0)
    nb = pl.num_programs(0)
    n_pages = pbuf_ref.shape[1]
    n_rows = k3_ref.shape[1]
    slot = b % 2
    group = 8

    def start_all(bb, sl):
        def body(p, c):
            _page_copy(cidx_ref, pbuf_ref, sem_ref, pt_ref, bb, p, sl).start()
            return c
        lax.fori_loop(0, n_pages, body, 0)

    @pl.when(b == 0)
    def _():
        start_all(0, 0)

    @pl.when(b + 1 < nb)
    def _():
        start_all(b + 1, 1 - slot)

    def wait_body(p, c):
        _page_copy(cidx_ref, pbuf_ref, sem_ref, pt_ref, b, p, slot).wait()
        return c
    lax.fori_loop(0, n_pages, wait_body, 0)

    q = q_ref[...]
    wsel = wsel_ref[...]
    trow = lax.broadcasted_iota(I32, (8, LANES), 0)

    def to_keys(kt):
        dots = jnp.maximum(jnp.dot(q, kt, preferred_element_type=F32), 0.0)
        sc = jnp.dot(wsel, dots.astype(BF16), preferred_element_type=F32)
        return _float_key(sc)

    def score_group(gi, c):
        base = gi * group
        kt = jnp.concatenate([pbuf_ref[slot, base + u] for u in range(group)], axis=1).astype(BF16)
        keys = to_keys(kt)
        for t in range(t_new):
            for u in range(group):
                k3_ref[t, pl.ds(base + u, 1), :] = keys[t:t + 1, u * LANES:(u + 1) * LANES]
        return c
    lax.fori_loop(0, n_pages // group, score_group, 0)

    knew = to_keys(knew_ref[...])
    lane = lax.broadcasted_iota(I32, (8, LANES), 1)
    knew = jnp.where((lane <= trow) & (lane < t_new), knew, INT_MIN)
    pad = jnp.full((n_rows - n_pages - 1, LANES), INT_MIN, I32)
    for t in range(t_new):
        k3_ref[t, n_pages:n_pages + 1, :] = knew[t:t + 1]
        k3_ref[t, n_pages + 1:, :] = pad

    def search_cond(s):
        it = s[0]
        live = s[1] < s[1 + t_new]
        for t in range(1, t_new):
            live = live | (s[1 + t] < s[1 + t_new + t])
        return (it < 40) & live

    def search_body(s):
        it = s[0]
        los, his = [], []
        for t in range(t_new):
            lo, hi = s[1 + t], s[1 + t_new + t]
            mid = _ceil_avg(lo, hi)
            c = jnp.sum(jnp.where(k3_ref[t] >= mid, 1, 0))
            ge = c >= topk
            los.append(jnp.where(ge, mid, lo))
            his.append(jnp.where(c == topk, mid, jnp.where(ge, hi, mid - 1)))
        return (it + 1, *los, *his)

    init = (jnp.int32(0),) + (jnp.int32(INT_MIN + 1),) * t_new + (jnp.int32(INT_MAX),) * t_new
    res = lax.while_loop(search_cond, search_body, init)

    rowid = lax.broadcasted_iota(I32, (n_rows, 1), 0).astype(F32)
    qlane = lax.broadcasted_iota(I32, (1, 2 * LANES), 1).astype(F32)
    liota = lax.broadcasted_iota(I32, (LANES, 2 * LANES), 0).astype(F32)
    lstrict = lstrict_ref[...]
    idx_ref[...] = jnp.zeros(idx_ref.shape, I32)
    past = lax.broadcasted_iota(I32, (n_rows, 1), 0) < n_pages
    kpos = (lax.broadcasted_iota(I32, (n_rows, LANES), 0) * LANES
            + lax.broadcasted_iota(I32, (n_rows, LANES), 1))
    for t in range(t_new):
        thr = res[1 + t]

        n_gt = jnp.sum(jnp.where(k3_ref[t] > thr, 1, 0))

        @pl.when(jnp.sum(jnp.where(k3_ref[t] >= thr, 1, 0)) > topk)
        def _():
            keep = topk - n_gt

            def pos_body(_, s):
                lo, hi = s
                mid = (lo + hi) >> 1
                ok = jnp.sum(jnp.where((k3_ref[t] == thr) & (kpos <= mid), 1, 0)) >= keep
                return jnp.where(ok, lo, mid + 1), jnp.where(ok, mid, hi)

            last, _ = lax.fori_loop(0, (n_rows * LANES - 1).bit_length(), pos_body,
                                    (jnp.int32(0), jnp.int32(n_rows * LANES - 1)))
            kt = k3_ref[t]
            k3_ref[t] = jnp.where((kt == thr) & (kpos > last), thr - 1, kt)

        sel = jnp.where((k3_ref[t] >= thr) & past, 1.0, 0.0)
        selb = sel.astype(BF16)
        cnt = jnp.sum(sel, axis=1, keepdims=True)
        start = jnp.dot(lstrict, jnp.broadcast_to(cnt, (n_rows, LANES)).astype(BF16),
                        preferred_element_type=F32)[:, :1]
        onehot = jnp.where((start <= qlane) & (qlane < start + cnt), 1.0, 0.0)
        nt = (((1,), (1,)), ((), ()))
        et = lax.dot_general(ut_ref[...], selb, nt, preferred_element_type=F32)
        selt = lax.dot_general(eye_ref[...], selb, nt, preferred_element_type=F32)
        ept = jnp.where(selt > 0.5, et, -1.0).astype(BF16)
        gt = jnp.dot(ept, onehot.astype(BF16), preferred_element_type=F32)
        start_q = jnp.sum(onehot * start, axis=0, keepdims=True)
        chunk_q = jnp.sum(onehot * rowid, axis=0, keepdims=True)
        lane_q = jnp.sum(jnp.where(gt == qlane - start_q, liota, 0.0), axis=0, keepdims=True)
        total = jnp.sum(cnt)
        ids = (chunk_q * LANES + lane_q).astype(I32)
        idx_ref[t:t + 1, :] = jnp.where(qlane < total, ids, -1)
        new_sel = jnp.where(k3_ref[t, n_pages:n_pages + 1, :] >= thr, 1, 0)
        idx_ref[4 + t:5 + t, :] = jnp.concatenate([new_sel, jnp.zeros_like(new_sel)], axis=1)


def _sample_select(page_table, cidx_t, q_s, wsel, knew_t, *, topk, t_new):
    db, n_pages = page_table.shape
    n_rows = ((n_pages + 1 + LANES - 1) // LANES) * LANES
    lstrict = jnp.asarray(np.tril(np.ones((n_rows, n_rows), np.float32), -1), BF16)
    ut = jnp.asarray(np.tril(np.ones((LANES, LANES), np.float32), -1), BF16)
    eye = jnp.asarray(np.eye(LANES, dtype=np.float32), BF16)
    const = lambda shape: pl.BlockSpec(shape, lambda b, pt: (0,) * len(shape))
    return pl.pallas_call(
        functools.partial(_sample_select_kernel, topk=topk, t_new=t_new),
        grid_spec=pltpu.PrefetchScalarGridSpec(
            num_scalar_prefetch=1, grid=(db,),
            in_specs=[pl.BlockSpec(memory_space=pl.ANY),
                      pl.BlockSpec((None,) + q_s.shape[1:], lambda b, pt: (b, 0, 0)),
                      pl.BlockSpec((None,) + wsel.shape[1:], lambda b, pt: (b, 0, 0)),
                      pl.BlockSpec((None,) + knew_t.shape[1:], lambda b, pt: (b, 0, 0)),
                      const((n_rows, n_rows)), const((LANES, LANES)), const((LANES, LANES))],
            out_specs=pl.BlockSpec((None, 8, 2 * LANES), lambda b, pt: (b, 0, 0)),
            scratch_shapes=[pltpu.VMEM((t_new, n_rows, LANES), I32),
                            pltpu.VMEM((2, n_pages, D_IDX, PAGE), F32),
                            pltpu.SemaphoreType.DMA((2,))]),
        out_shape=jax.ShapeDtypeStruct((db, 8, 2 * LANES), I32),
        compiler_params=_cparams(1),
        name="sample_select",
    )(page_table, cidx_t, q_s, wsel, knew_t, lstrict, ut, eye)


def _sample_attend_kernel(idx_s_ref, pt_ref, rb_ref, kva_ref, kvnew_ref, idxv_ref, q_ref, o_ref,
                          gbuf_ref, nbuf_ref, sem_ref, *, t_new, n_past):
    b = pl.program_id(0)
    nsel = idxv_ref.shape[1]
    rows = 2 * HKV_A
    unroll = 8

    new_rows = t_new * rows
    cp_new = pltpu.make_async_copy(kvnew_ref.at[pl.ds(b * new_rows, new_rows)],
                                   nbuf_ref.at[pl.ds(0, new_rows)], sem_ref.at[1])
    cp_new.start()
    nbuf_ref[pl.ds(new_rows, nbuf_ref.shape[0] - new_rows), :] = jnp.zeros(
        (nbuf_ref.shape[0] - new_rows, HEAD_DIM), F32)

    def row_copy(t, qi):
        i = jnp.maximum(idx_s_ref[b * t_new + t, qi], 0)
        page = lax.shift_right_logical(i, PAGE.bit_length() - 1)
        src = (pt_ref[b, page] * PAGE + (i & (PAGE - 1))) * rows
        return pltpu.make_async_copy(kva_ref.at[pl.ds(pl.multiple_of(src, rows), rows)],
                                     gbuf_ref.at[t, pl.ds(pl.multiple_of(qi * rows, rows), rows)],
                                     sem_ref.at[0])

    for t in range(t_new):
        def issue(qo, c):
            for u in range(unroll):
                row_copy(t, qo * unroll + u).start()
            return c
        lax.fori_loop(0, nsel // unroll, issue, 0)
    for t in range(t_new):
        def drain(qo, c):
            for u in range(unroll):
                row_copy(t, qo * unroll + u).wait()
            return c
        lax.fori_loop(0, nsel // unroll, drain, 0)
    cp_new.wait()

    nlane = lax.broadcasted_iota(I32, (1, LANES), 1)
    zpad = jnp.zeros((LANES - 8, HEAD_DIM), BF16)
    nt = (((1,), (1,)), ((), ()))
    for t in range(t_new):
        ids = idxv_ref[t:t + 1, :]
        valid = ids >= 0
        dist = (n_past + t) - ids
        new_valid = (idxv_ref[4 + t:5 + t, :LANES] > 0) & (nlane < t_new)
        new_dist = t - nlane
        for g in range(HKV_A):
            k = gbuf_ref[t, pl.ds(g, nsel, stride=rows), :].astype(BF16)
            v = gbuf_ref[t, pl.ds(HKV_A + g, nsel, stride=rows), :].astype(BF16)
            kn = jnp.concatenate([nbuf_ref[pl.ds(g, 8, stride=rows), :].astype(BF16), zpad], axis=0)
            vn = jnp.concatenate([nbuf_ref[pl.ds(HKV_A + g, 8, stride=rows), :].astype(BF16), zpad], axis=0)
            qg = q_ref[t, g]
            srow = lax.broadcasted_iota(I32, (8, 1), 0)
            s = lax.dot_general(qg, k, nt, preferred_element_type=F32) * SCALE
            s = s + jnp.where(srow == 0, _rel_bias_of(dist, rb_ref, 2 * g), _rel_bias_of(dist, rb_ref, 2 * g + 1))
            s = jnp.where(valid, s, -jnp.inf)
            sn = lax.dot_general(qg, kn, nt, preferred_element_type=F32) * SCALE
            sn = sn + jnp.where(srow == 0, _rel_bias_of(new_dist, rb_ref, 2 * g),
                                _rel_bias_of(new_dist, rb_ref, 2 * g + 1))
            sn = jnp.where(new_valid, sn, -jnp.inf)
            mx = jnp.maximum(jnp.max(s, axis=1, keepdims=True), jnp.max(sn, axis=1, keepdims=True))
            p = jnp.exp(s - mx)
            pn = jnp.exp(sn - mx)
            den = jnp.sum(p, axis=1, keepdims=True) + jnp.sum(pn, axis=1, keepdims=True)
            p = p / den
            pn = pn / den
            out = (jnp.dot(p.astype(BF16), v, preferred_element_type=F32)
                   + jnp.dot(pn.astype(BF16), vn, preferred_element_type=F32))
            o_ref[t:t + 1, (2 * g) * HEAD_DIM:(2 * g + 1) * HEAD_DIM] = out[0:1]
            o_ref[t:t + 1, (2 * g + 1) * HEAD_DIM:(2 * g + 2) * HEAD_DIM] = out[1:2]
    o_ref[t_new:, :] = jnp.zeros((8 - t_new, W_A), F32)


def _sample_attend(idx, page_table, rel_bias, kva_rows, kvnew_rows, q_pad, *, t_new, n_past):
    db = page_table.shape[0]
    nsel = idx.shape[-1]
    idx_s = idx[:, :t_new].reshape(db * t_new, nsel)
    return pl.pallas_call(
        functools.partial(_sample_attend_kernel, t_new=t_new, n_past=n_past),
        grid_spec=pltpu.PrefetchScalarGridSpec(
            num_scalar_prefetch=3, grid=(db,),
            in_specs=[pl.BlockSpec(memory_space=pl.ANY), pl.BlockSpec(memory_space=pl.ANY),
                      pl.BlockSpec((None, 8, nsel), lambda b, *_: (b, 0, 0)),
                      pl.BlockSpec((None,) + q_pad.shape[1:], lambda b, *_: (b, 0, 0, 0, 0))],
            out_specs=pl.BlockSpec((None, 8, W_A), lambda b, *_: (b, 0, 0)),
            scratch_shapes=[pltpu.VMEM((t_new, nsel * 2 * HKV_A, HEAD_DIM), F32),
                            pltpu.VMEM((8 * 2 * HKV_A, HEAD_DIM), F32),
                            pltpu.SemaphoreType.DMA((2,))]),
        out_shape=jax.ShapeDtypeStruct((db, 8, W_A), F32),
        compiler_params=_cparams(1),
        name="sample_attend",
    )(idx_s, page_table, rel_bias, kva_rows, kvnew_rows, idx, q_pad)


def _sb_sample_kernel(pt_ref, kvb_ref, kvnew_ref, q_ref, u_ref, o_ref, pbuf_ref, acc_ref, carry_ref,
                      sem_ref, *, t_new):
    b = pl.program_id(0)
    n_pages = pt_ref.shape[1]
    prow = pbuf_ref.shape[1]
    step = 2 * H_B
    u = u_ref[...]
    trow = lax.broadcasted_iota(I32, (8, PAGE), 0)
    scol = lax.broadcasted_iota(I32, (8, PAGE), 1)
    live_rows = lax.broadcasted_iota(I32, (8, 1), 0) < t_new

    def page_copy(p, slot):
        return pltpu.make_async_copy(kvb_ref.at[pl.ds(pt_ref[b, p] * prow, prow)], pbuf_ref.at[slot],
                                     sem_ref.at[slot])

    def process(slot, mask):
        for h in range(H_B):
            k = pbuf_ref[slot, pl.ds(h, PAGE, stride=step), :].astype(BF16)
            v = pbuf_ref[slot, pl.ds(H_B + h, PAGE, stride=step), :].astype(BF16)
            contrib, carry = _sb_block(q_ref[h], k, v, u, mask, carry_ref[h])
            acc_ref[h] += contrib
            carry_ref[h] = carry

    page_copy(n_pages - 1, 0).start()
    new_rows = t_new * step
    cp_new = pltpu.make_async_copy(kvnew_ref.at[pl.ds(b * new_rows, new_rows)],
                                   pbuf_ref.at[1, pl.ds(0, new_rows)], sem_ref.at[1])
    cp_new.start()
    pbuf_ref[1, pl.ds(new_rows, prow - new_rows), :] = jnp.zeros((prow - new_rows, HEAD_DIM), F32)
    acc_ref[...] = jnp.zeros(acc_ref.shape, F32)
    carry_ref[...] = jnp.zeros(carry_ref.shape, F32)
    cp_new.wait()
    process(1, (scol < trow) & (scol < t_new))

    def alive():
        c = jnp.where(live_rows, carry_ref[...], -jnp.inf)
        return jnp.max(c) > SB_DEAD

    def cond(s):
        i, go = s
        return (i < n_pages) & go

    def body(s):
        i, _ = s
        slot = i % 2
        page_copy(n_pages - 1 - i, slot).wait()

        @pl.when(i + 1 < n_pages)
        def _():
            page_copy(n_pages - 2 - i, 1 - slot).start()
        process(slot, None)
        return i + 1, alive()

    i_end, _ = lax.while_loop(cond, body, (jnp.int32(0), alive()))

    @pl.when(i_end < n_pages)
    def _():
        page_copy(n_pages - 1 - i_end, i_end % 2).wait()

    for h in range(H_B):
        o_ref[:, h * HEAD_DIM:(h + 1) * HEAD_DIM] = acc_ref[h]


def _sb_sample(page_table, kvb_rows, kvnew_rows, q_pad, u_strict, *, t_new):
    db = page_table.shape[0]
    prow = PAGE * 2 * H_B
    return pl.pallas_call(
        functools.partial(_sb_sample_kernel, t_new=t_new),
        grid_spec=pltpu.PrefetchScalarGridSpec(
            num_scalar_prefetch=1, grid=(db,),
            in_specs=[pl.BlockSpec(memory_space=pl.ANY), pl.BlockSpec(memory_space=pl.ANY),
                      pl.BlockSpec((None, H_B, 8, HEAD_DIM), lambda b, pt: (b, 0, 0, 0)),
                      pl.BlockSpec((PAGE, PAGE), lambda b, pt: (0, 0))],
            out_specs=pl.BlockSpec((None, 8, W_B), lambda b, pt: (b, 0, 0)),
            scratch_shapes=[pltpu.VMEM((2, prow, HEAD_DIM), F32),
                            pltpu.VMEM((H_B, 8, HEAD_DIM), F32),
                            pltpu.VMEM((H_B, 8, 1), F32),
                            pltpu.SemaphoreType.DMA((2,))]),
        out_shape=jax.ShapeDtypeStruct((db, 8, W_B), F32),
        compiler_params=_cparams(1),
        name="sb_sample",
    )(page_table, kvb_rows, kvnew_rows, q_pad, u_strict)


def _strict_upper(n):
    return jnp.asarray(np.tril(np.ones((n, n), np.float32), -1), BF16)


def kernel(x_prompt, x_sample, cache_kv_a, cache_idx, cache_kv_b, page_table, c_prompt, c_sample, rel_bias, w_ada, b_ada, g_pre_mix, w_in, g_grp_a, g_grp_b, w_out, g_post_mix, g_pre_ffn, w_gate_up, w_down, g_post_ffn):
    nb, s, d = x_prompt.shape
    db, t_new, _ = x_sample.shape
    n_pages = page_table.shape[1]
    n_past = n_pages * PAGE
    assert w_ada.shape[0] == 1 and d == (H_A + H_B) * HEAD_DIM
    assert cache_kv_a.shape[2:] == (PAGE, 2, HKV_A, HEAD_DIM) and t_new <= 4

    sp = np.cumsum([0, W_A, KV_A, KV_A, H_IDX * D_IDX, D_IDX, H_IDX, W_B, W_B, W_B])
    wi = w_in[0]
    col = lambda i: wi[:, sp[i]:sp[i + 1]]
    w_main = jnp.concatenate([col(0), col(1), col(2), col(3), col(6), col(7), col(8)], axis=1).astype(BF16)
    w_small = jnp.concatenate([col(4), col(5), jnp.zeros((d, LANES - D_IDX - H_IDX), F32)], axis=1).astype(BF16)
    w_small_t = w_small.T
    w_out_b = w_out[0].astype(BF16)
    w_gu_b = w_gate_up[0].astype(BF16)
    w_down_b = w_down[0].astype(BF16)

    mod = _modulation(jnp.concatenate([c_prompt, c_sample], axis=0), w_ada[0], b_ada)
    mod6 = mod.reshape(nb + db, 6, d).transpose(1, 0, 2)
    mp = [mod6[i, :nb].reshape(nb, 1, d) for i in range(6)]
    ms = [jnp.repeat(mod6[i, nb:], t_new, axis=0)[None] for i in range(6)]

    tq = 256
    topk_p = min(TOPK_MAX, s // 4)
    xp = x_prompt.reshape(nb * s, d)
    (qa, kva, kvab, qi, qb, kvb, kvbb, sm, kt) = _project(
        xp, g_pre_mix, mp[0], mp[1], w_main, w_small, w_small_t, tm=min(512, s), rows_per_group=s)
    bias_tiles = _bias_tiles(rel_bias, tq)
    oa = _dsa_prompt(rel_bias, qi, sm, qa, kvab, bias_tiles, nb=nb, s=s, tq=tq, topk=topk_p)
    ob = _sb_prompt(qb, kvbb, _strict_upper(tq), nb=nb, s=s, tq=tq, heads=2)
    x1, h2 = _merge(oa, ob, g_grp_a, g_grp_b, w_out_b, xp, g_post_mix, mp[2], g_pre_ffn, mp[3], mp[4],
                    tm=256, rows_per_group=s)
    y_prompt = _ffn(h2, w_gu_b, w_down_b, x1, g_post_ffn, mp[5], tm=512, tf=512, td=512, rows_per_group=s)

    ms_rows = db * t_new
    topk_s = min(TOPK_MAX, (n_past + t_new) // 4)
    xs = x_sample.reshape(ms_rows, d)
    (qa_s, kva_s, _, qi_s, qb_s, kvb_s, _, sm_s, kt_s) = _project(
        xs, g_pre_mix, ms[0], ms[1], w_main, w_small, w_small_t, tm=ms_rows, rows_per_group=ms_rows)

    q_s = qi_s.reshape(db, t_new * H_IDX, D_IDX)
    w_s = sm_s[:, D_IDX:D_IDX + H_IDX].reshape(db, t_new, H_IDX)
    wsel = (jnp.eye(8, t_new, dtype=F32)[None, :, :, None] * w_s[:, None]).reshape(db, 8, t_new * H_IDX)
    knew_t = jnp.pad(kt_s[0].reshape(D_IDX, db, t_new).transpose(1, 0, 2),
                     ((0, 0), (0, 0), (0, LANES - t_new))).astype(BF16)
    cidx_t = jnp.swapaxes(cache_idx[0], 1, 2)
    idx = _sample_select(page_table, cidx_t, q_s, wsel.astype(BF16), knew_t, topk=topk_s, t_new=t_new)

    kva_rows = cache_kv_a.reshape(-1, HEAD_DIM)
    kvnew_rows = kva_s
    qa4 = qa_s.reshape(db, t_new, HKV_A, H_A // HKV_A, HEAD_DIM)
    q_pad = jnp.pad(qa4, ((0, 0), (0, 0), (0, 0), (0, 8 - H_A // HKV_A), (0, 0)))
    oa_s = _sample_attend(idx, page_table, rel_bias, kva_rows, kvnew_rows, q_pad, t_new=t_new, n_past=n_past)

    kvb_rows = cache_kv_b.reshape(-1, HEAD_DIM)
    kvbnew_rows = kvb_s
    qb_pad = jnp.pad(qb_s.reshape(db, t_new, H_B, HEAD_DIM).transpose(0, 2, 1, 3),
                     ((0, 0), (0, 0), (0, 8 - t_new), (0, 0)))
    ob_s = _sb_sample(page_table, kvb_rows, kvbnew_rows, qb_pad, _strict_upper(PAGE), t_new=t_new)

    oa_s = oa_s[:, :t_new].reshape(ms_rows, W_A)
    ob_s = ob_s[:, :t_new].reshape(ms_rows, W_B)
    x1_s, h2_s = _merge(oa_s, ob_s, g_grp_a, g_grp_b, w_out_b, xs, g_post_mix, ms[2], g_pre_ffn, ms[3],
                        ms[4], tm=ms_rows, rows_per_group=ms_rows)
    y_sample = _ffn(h2_s, w_gu_b, w_down_b, x1_s, g_post_ffn, ms[5], tm=ms_rows, tf=512, td=512,
                    rows_per_group=ms_rows)

    return (y_prompt.reshape(nb, s, d),
            y_sample.reshape(db, t_new, d),
            kva.reshape(1, nb, s, 2, HKV_A, HEAD_DIM),
            jnp.swapaxes(kt, 1, 2)[None],
            kvb.reshape(1, nb, s, 2, H_B, HEAD_DIM),
            kva_s.reshape(1, db, t_new, 2, HKV_A, HEAD_DIM),
            sm_s[:, :D_IDX].reshape(1, db, t_new, D_IDX),
            kvb_s.reshape(1, db, t_new, 2, H_B, HEAD_DIM))
```

```python
import functools
import math

import jax
import jax.numpy as jnp
import numpy as np
from jax import lax
from jax.experimental import pallas as pl
from jax.experimental.pallas import tpu as pltpu

F32 = jnp.float32
BF16 = jnp.bfloat16
I32 = jnp.int32

HEAD_DIM = 128
H_A = 8
HKV_A = 4
H_B = 8
H_IDX = 16
D_IDX = 64
TOPK_MAX = 256
N_BUCKETS = 32
MAX_DISTANCE = 128
PAGE = 128
EPS = 1e-6
SCALE = HEAD_DIM ** -0.5
IDX_W_SCALE = (H_IDX * D_IDX) ** -0.5
W_A = H_A * HEAD_DIM
KV_A = HKV_A * HEAD_DIM
W_B = H_B * HEAD_DIM
INT_MIN = -2 ** 31
INT_MAX = 2 ** 31 - 1
SB_DEAD = -120.0

LANES = 128
VMEM_LIMIT = 56 * 1024 * 1024


def _cparams(n_axes, vmem=VMEM_LIMIT):
    return pltpu.CompilerParams(dimension_semantics=("arbitrary",) * n_axes,
                                vmem_limit_bytes=vmem)


def _bucket_thresholds():
    max_exact = N_BUCKETS // 2
    d = np.arange(0, 4 * MAX_DISTANCE, dtype=np.int32)
    df = np.maximum(d, 1).astype(np.float32)
    large = max_exact + (np.log(df / np.float32(max_exact)) / np.float32(math.log(MAX_DISTANCE / max_exact))
                         * np.float32(N_BUCKETS - max_exact)).astype(np.int32)
    large = np.minimum(large, N_BUCKETS - 1)
    bucket = np.where(d < max_exact, d, large)
    assert bucket[-1] == N_BUCKETS - 1 and np.all(np.diff(bucket) >= 0)
    return [int(np.argmax(bucket >= b)) for b in range(1, N_BUCKETS)]


_BUCKET_THR = _bucket_thresholds()


def _rel_bias_of(dist, rb_ref, h):
    val = jnp.full(dist.shape, rb_ref[0, h], F32)
    for b in range(1, N_BUCKETS):
        val = jnp.where(dist >= _BUCKET_THR[b - 1], rb_ref[b, h], val)
    return val


def _float_key(x):
    bits = lax.bitcast_convert_type(x, I32)
    return bits ^ ((bits >> 31) & INT_MAX)


def _ceil_avg(lo, hi):
    x = lo ^ hi
    return (lo & hi) + (x >> 1) + (x & 1)


def _rms(x, g):
    return x * lax.rsqrt(jnp.mean(x * x, axis=-1, keepdims=True) + EPS) * g


def _mod_kernel(c_ref, w_ref, b_ref, o_ref):
    c = c_ref[...]
    a = c * jax.nn.sigmoid(c)
    o_ref[...] = jnp.dot(a, w_ref[...], preferred_element_type=F32,
                         precision=lax.Precision.HIGHEST) + b_ref[...]


def _modulation(c_all, w_ada, b_ada):
    r, d = c_all.shape
    n = w_ada.shape[1]
    tn = 1024
    return pl.pallas_call(
        _mod_kernel,
        grid=(n // tn,),
        in_specs=[pl.BlockSpec((r, d), lambda i: (0, 0)),
                  pl.BlockSpec((d, tn), lambda i: (0, i)),
                  pl.BlockSpec((1, tn), lambda i: (0, i))],
        out_specs=pl.BlockSpec((r, tn), lambda i: (0, i)),
        out_shape=jax.ShapeDtypeStruct((r, n), F32),
        compiler_params=_cparams(1),
        name="adaln_mod",
    )(c_all, w_ada, b_ada)


PROJ_TN = 1024
_PROJ_GROUPS = {"qa": (0, 1), "kva": (1, 1), "qi": (2, 1), "qb": (3, 1), "kvb": (4, 2)}
N_MAIN = 6


def _proj_kernel(x_ref, g_ref, sh_ref, sc_ref, wm_ref, ws_ref, wst_ref,
                 qa_ref, kva_ref, kvab_ref, qi_ref, qb_ref, kvb_ref, kvbb_ref,
                 sm_ref, kt_ref, h_ref):
    n = pl.program_id(1)

    @pl.when(n == 0)
    def _():
        y = _rms(x_ref[...], g_ref[...])
        h_ref[...] = (y * (1.0 + sc_ref[...]) + sh_ref[...]).astype(BF16)

    def mm():
        return jnp.dot(h_ref[...], wm_ref[...], preferred_element_type=F32)

    def in_group(name):
        first, count = _PROJ_GROUPS[name]
        return (n >= first) & (n < first + count)

    @pl.when(in_group("qa"))
    def _():
        qa_ref[...] = mm().astype(BF16)

    def store_heads(out_ref, r, first_row, rows_per_token):
        for j in range(r.shape[1] // HEAD_DIM):
            out_ref[pl.ds(first_row + j, r.shape[0], stride=rows_per_token), :] = r[:, j * HEAD_DIM:(j + 1) * HEAD_DIM]

    @pl.when(in_group("kva"))
    def _():
        r = mm()
        store_heads(kva_ref, r, 0, 2 * HKV_A)
        kvab_ref[...] = r.astype(BF16)

    @pl.when(in_group("qi"))
    def _():
        qi_ref[...] = mm().astype(BF16)

    @pl.when(in_group("qb"))
    def _():
        qb_ref[...] = mm().astype(BF16)

    for part in range(_PROJ_GROUPS["kvb"][1]):
        @pl.when(n == _PROJ_GROUPS["kvb"][0] + part)
        def _():
            r = mm()
            store_heads(kvb_ref, r, part * H_B, 2 * H_B)
            kvbb_ref[...] = r.astype(BF16)

    @pl.when(n == N_MAIN)
    def _():
        r = jnp.dot(h_ref[...], ws_ref[...], preferred_element_type=F32)
        lane = lax.broadcasted_iota(I32, r.shape, 1)
        is_w = (lane >= D_IDX) & (lane < D_IDX + H_IDX)
        sm_ref[...] = jnp.where(is_w, r * IDX_W_SCALE, r)
        rt = lax.dot_general(wst_ref[...], h_ref[...], (((1,), (1,)), ((), ())),
                             preferred_element_type=F32)
        kt_ref[...] = rt[:D_IDX]


def _project(x2d, g_pre, shift, scale, w_main, w_small, w_small_t, *, tm, rows_per_group):
    m, d = x2d.shape
    nb = m // tm
    bpg = rows_per_group // tm
    r = shift.shape[1]
    ng = m // rows_per_group
    tn = PROJ_TN
    grid = (nb, N_MAIN + 1)
    row = lambda i, n: (i, 0)
    grp = lambda i, n: (i // bpg, 0, 0)

    def tile(name):
        first, count = _PROJ_GROUPS[name]
        return pl.BlockSpec((tm, tn), lambda i, n: (i, jnp.clip(n - first, 0, count - 1)))

    out_shape = [
        jax.ShapeDtypeStruct((m, W_A), BF16),
        jax.ShapeDtypeStruct((m * 2 * HKV_A, HEAD_DIM), F32),
        jax.ShapeDtypeStruct((m, 2 * KV_A), BF16),
        jax.ShapeDtypeStruct((m, H_IDX * D_IDX), BF16),
        jax.ShapeDtypeStruct((m, W_B), BF16),
        jax.ShapeDtypeStruct((m * 2 * H_B, HEAD_DIM), F32),
        jax.ShapeDtypeStruct((m, 2 * W_B), BF16),
        jax.ShapeDtypeStruct((m, LANES), F32),
        jax.ShapeDtypeStruct((ng, D_IDX, rows_per_group), F32),
    ]
    out_specs = [
        tile("qa"), pl.BlockSpec((tm * 2 * HKV_A, HEAD_DIM), row), tile("kva"), tile("qi"), tile("qb"),
        pl.BlockSpec((tm * 2 * H_B, HEAD_DIM), row), tile("kvb"),
        pl.BlockSpec((tm, LANES), row),
        pl.BlockSpec((None, D_IDX, tm), lambda i, n: (i // bpg, 0, i % bpg)),
    ]
    in_specs = [
        pl.BlockSpec((tm, d), row),
        pl.BlockSpec((1, d), lambda i, n: (0, 0)),
        pl.BlockSpec((None, r, d), grp),
        pl.BlockSpec((None, r, d), grp),
        pl.BlockSpec((d, tn), lambda i, n: (0, jnp.minimum(n, N_MAIN - 1))),
        pl.BlockSpec((d, LANES), lambda i, n: (0, 0)),
        pl.BlockSpec((LANES, d), lambda i, n: (0, 0)),
    ]
    return pl.pallas_call(
        _proj_kernel,
        grid=grid, in_specs=in_specs, out_specs=out_specs, out_shape=out_shape,
        scratch_shapes=[pltpu.VMEM((tm, d), BF16)],
        compiler_params=_cparams(2),
        name="prenorm_proj",
    )(x2d, g_pre, shift, scale, w_main, w_small, w_small_t)


def _bias_tiles_kernel(rb_ref, o_ref):
    h = pl.program_id(0)
    tq = o_ref.shape[-1]
    key = lax.broadcasted_iota(I32, (tq, tq), 0)
    qry = lax.broadcasted_iota(I32, (tq, tq), 1)
    for k in range(2):
        o_ref[k] = _rel_bias_of(qry - key + k * tq, rb_ref, h)


def _bias_tiles(rel_bias, tq):
    return pl.pallas_call(
        _bias_tiles_kernel,
        grid=(H_A,),
        in_specs=[pl.BlockSpec(memory_space=pltpu.SMEM)],
        out_specs=pl.BlockSpec((2, None, tq, tq), lambda h: (0, h, 0, 0)),
        out_shape=jax.ShapeDtypeStruct((2, H_A, tq, tq), F32),
        compiler_params=_cparams(1),
        name="t5_bias_tiles",
    )(rel_bias)


GROUPS_PER_PASS = 4


def _dsa_prompt_kernel(rb_ref, qi_ref, smq_ref, smk_ref, qa_ref, kv_ref, bias_ref,
                       o_ref, keys_ref, kab_ref, vt_ref, qit_ref, qat_ref, m_ref, l_ref, acc_ref, *, topk):
    j = pl.program_id(1)
    tq = qa_ref.shape[0]
    nq = keys_ref.shape[0]
    nkc = j + 1
    qpos = j * tq + lax.broadcasted_iota(I32, (1, tq), 1)

    @pl.when(j == 0)
    def _():
        def prep(c, carry):
            off = pl.multiple_of(c * tq, tq)
            r = smk_ref[pl.ds(off, tq), :]
            lane = lax.broadcasted_iota(I32, r.shape, 1)
            ka = jnp.where(lane < D_IDX, r, 0.0)
            kb = jnp.where(lane >= D_IDX, pltpu.roll(r, D_IDX, axis=1), 0.0)
            kab_ref[c] = jnp.concatenate([ka, kb], axis=0).astype(BF16)
            for g in range(HKV_A):
                v = kv_ref[pl.ds(off, tq), KV_A + g * HEAD_DIM:KV_A + (g + 1) * HEAD_DIM]
                vt_ref[c, g] = v.astype(F32).T.astype(BF16)
            return carry

        lax.fori_loop(0, nq, prep, 0)

    for hp in range(H_IDX // 2):
        qit_ref[hp] = qi_ref[:, hp * LANES:(hp + 1) * LANES].astype(F32).T.astype(BF16)
    for h in range(H_A):
        qat_ref[h] = qa_ref[:, h * HEAD_DIM:(h + 1) * HEAD_DIM].astype(F32).T.astype(BF16)
    wt = smq_ref[...].T[D_IDX:D_IDX + H_IDX]

    def score_chunk(c, carry):
        kab = kab_ref[c]
        acc = jnp.zeros((tq, tq), F32)
        for hp in range(H_IDX // 2):
            d = jnp.maximum(jnp.dot(kab, qit_ref[hp], preferred_element_type=F32), 0.0)
            acc = acc + wt[2 * hp:2 * hp + 1] * d[:tq] + wt[2 * hp + 1:2 * hp + 2] * d[tq:]
        kpos = c * tq + lax.broadcasted_iota(I32, (tq, 1), 0)
        keys_ref[c] = jnp.where(kpos <= qpos, _float_key(acc), INT_MIN)
        return carry

    lax.fori_loop(0, nkc, score_chunk, 0)

    need = (qpos + 1) > topk

    def count_ge(mid):
        def body(c, cnt):
            hit = jnp.where(keys_ref[c] >= mid, 1, 0)
            return cnt + jnp.sum(hit.reshape(tq // 8, 8, tq), axis=0)

        cnt = lax.fori_loop(0, nkc, body, jnp.zeros((8, tq), I32))
        return jnp.sum(cnt, axis=0, keepdims=True)

    def search_body(_, s):
        lo, hi = s
        mid = _ceil_avg(lo, hi)
        c = count_ge(mid)
        ge = c >= topk
        return jnp.where(ge, mid, lo), jnp.where(c == topk, mid, jnp.where(ge, hi, mid - 1))

    lo0 = jnp.full((1, tq), INT_MIN + 1, I32)
    hi0 = jnp.where(need, INT_MAX, INT_MIN + 1)
    thr, _ = lax.fori_loop(0, 32, search_body, (lo0, hi0))

    n_gt = count_ge(thr + 1)
    excess = count_ge(thr) > topk

    @pl.when(jnp.max(jnp.where(excess, 1.0, 0.0)) > 0.5)
    def _():
        keep = topk - n_gt

        def tied_upto(pos):
            def body(c, cnt):
                kpos = c * tq + lax.broadcasted_iota(I32, (tq, 1), 0)
                hit = jnp.where((keys_ref[c] == thr) & (kpos <= pos), 1, 0)
                return cnt + jnp.sum(hit.reshape(tq // 8, 8, tq), axis=0)

            cnt = lax.fori_loop(0, nkc, body, jnp.zeros((8, tq), I32))
            return jnp.sum(cnt, axis=0, keepdims=True)

        def pos_body(_, s):
            lo, hi = s
            mid = (lo + hi) >> 1
            ok = tied_upto(mid) >= keep
            return jnp.where(ok, lo, mid + 1), jnp.where(ok, mid, hi)

        last, _ = lax.fori_loop(0, (nq * tq - 1).bit_length(), pos_body,
                                (jnp.zeros((1, tq), I32), jnp.full((1, tq), nq * tq - 1, I32)))

        def demote(c, carry):
            kc = keys_ref[c]
            kpos = c * tq + lax.broadcasted_iota(I32, (tq, 1), 0)
            keys_ref[c] = jnp.where(excess & (kc == thr) & (kpos > last), thr - 1, kc)
            return carry

        lax.fori_loop(0, nkc, demote, 0)

    far = N_BUCKETS - 1
    left = lax.broadcasted_iota(I32, (1, 2 * tq), 1) < tq
    for g0 in range(0, HKV_A, GROUPS_PER_PASS):
        groups = range(g0, g0 + GROUPS_PER_PASS)
        qt2 = [jnp.concatenate([qat_ref[2 * g], qat_ref[2 * g + 1]], axis=1) for g in groups]
        m_ref[...] = jnp.full(m_ref.shape, -jnp.inf, F32)
        l_ref[...] = jnp.zeros(l_ref.shape, F32)
        acc_ref[...] = jnp.zeros(acc_ref.shape, F32)

        def block(c, bias_of):
            off = pl.multiple_of(c * tq, tq)
            sel = keys_ref[c] >= thr
            sel2 = jnp.concatenate([sel, sel], axis=1)
            for i, g in enumerate(groups):
                k = kv_ref[pl.ds(off, tq), g * HEAD_DIM:(g + 1) * HEAD_DIM]
                s = jnp.dot(k, qt2[i], preferred_element_type=F32) * SCALE + bias_of(g)
                s = jnp.where(sel2, s, -jnp.inf)
                m_old = m_ref[i]
                m_new = jnp.maximum(m_old, jnp.max(s, axis=0, keepdims=True))
                m_safe = jnp.where(m_new == -jnp.inf, 0.0, m_new)
                p = jnp.exp(s - m_safe)
                alpha = jnp.exp(m_old - m_safe)
                l_ref[i] = alpha * l_ref[i] + jnp.sum(p, axis=0, keepdims=True)
                acc_ref[i] = alpha * acc_ref[i] + jnp.dot(vt_ref[c, g], p.astype(BF16),
                                                          preferred_element_type=F32)
                m_ref[i] = m_new

        far_bias = {g: jnp.where(left, rb_ref[far, 2 * g], rb_ref[far, 2 * g + 1]) for g in groups}

        def far_block(c, carry):
            block(c, lambda g: far_bias[g])
            return carry

        lax.fori_loop(0, j - 1, far_block, 0)

        def near_bias(k):
            return lambda g: jnp.concatenate([bias_ref[k, 2 * g], bias_ref[k, 2 * g + 1]], axis=1)

        @pl.when(j >= 1)
        def _():
            block(j - 1, near_bias(1))

        block(j, near_bias(0))

        for i, g in enumerate(groups):
            out = acc_ref[i] / l_ref[i]
            o_ref[:, (2 * g) * HEAD_DIM:(2 * g + 1) * HEAD_DIM] = out[:, :tq].T
            o_ref[:, (2 * g + 1) * HEAD_DIM:(2 * g + 2) * HEAD_DIM] = out[:, tq:].T


def _dsa_prompt(rel_bias, qi, sm, qa, kvab, bias_tiles, *, nb, s, tq, topk):
    nq = s // tq
    m = nb * s
    qrow = lambda b, j, rb: (b * nq + j, 0)
    in_specs = [
        pl.BlockSpec((tq, H_IDX * D_IDX), qrow),
        pl.BlockSpec((tq, LANES), qrow),
        pl.BlockSpec((s, LANES), lambda b, j, rb: (b, 0)),
        pl.BlockSpec((tq, W_A), qrow),
        pl.BlockSpec((s, 2 * KV_A), lambda b, j, rb: (b, 0)),
        pl.BlockSpec((2, H_A, tq, tq), lambda b, j, rb: (0, 0, 0, 0)),
    ]
    return pl.pallas_call(
        functools.partial(_dsa_prompt_kernel, topk=topk),
        grid_spec=pltpu.PrefetchScalarGridSpec(
            num_scalar_prefetch=1, grid=(nb, nq), in_specs=in_specs,
            out_specs=pl.BlockSpec((tq, W_A), qrow),
            scratch_shapes=[pltpu.VMEM((nq, tq, tq), I32),
                            pltpu.VMEM((nq, 2 * tq, LANES), BF16),
                            pltpu.VMEM((nq, HKV_A, HEAD_DIM, tq), BF16),
                            pltpu.VMEM((H_IDX // 2, LANES, tq), BF16),
                            pltpu.VMEM((H_A, HEAD_DIM, tq), BF16),
                            pltpu.VMEM((GROUPS_PER_PASS, 1, 2 * tq), F32),
                            pltpu.VMEM((GROUPS_PER_PASS, 1, 2 * tq), F32),
                            pltpu.VMEM((GROUPS_PER_PASS, HEAD_DIM, 2 * tq), F32)]),
        out_shape=jax.ShapeDtypeStruct((m, W_A), F32),
        compiler_params=_cparams(2),
        name="dsa_prompt",
    )(rel_bias, qi, sm, sm, qa, kvab, bias_tiles)


def _sb_block(q, k, v, u_strict, mask, carry):
    z = lax.dot_general(q, k, (((1,), (1,)), ((), ())), preferred_element_type=F32) * SCALE
    l1p = jnp.log1p(jnp.exp(-jnp.abs(z)))
    ls_pos = jnp.minimum(z, 0.0) - l1p
    ls_neg = jnp.minimum(-z, 0.0) - l1p
    lb1 = ls_neg if mask is None else jnp.where(mask, ls_neg, 0.0)
    hi = lb1.astype(BF16)
    lo = (lb1 - hi.astype(F32)).astype(BF16)
    suffix = (jnp.dot(hi, u_strict, preferred_element_type=F32)
              + jnp.dot(lo, u_strict, preferred_element_type=F32)) + carry
    a = jnp.exp(ls_pos + suffix)
    if mask is not None:
        a = jnp.where(mask, a, 0.0)
    contrib = jnp.dot(a.astype(BF16), v, preferred_element_type=F32)
    return contrib, carry + jnp.sum(lb1, axis=1, keepdims=True)


def _sb_prompt_kernel(q_ref, k_ref, v_ref, u_ref, o_ref, acc_ref, carry_ref, *, heads):
    j = pl.program_id(2)
    tq = q_ref.shape[0]
    u = u_ref[...]
    row = lax.broadcasted_iota(I32, (tq, tq), 0)
    col = lax.broadcasted_iota(I32, (tq, tq), 1)
    hs = lambda h: slice(h * HEAD_DIM, (h + 1) * HEAD_DIM)

    off = pl.multiple_of(j * tq, tq)
    for h in range(heads):
        contrib, carry = _sb_block(q_ref[:, hs(h)], k_ref[pl.ds(off, tq), hs(h)], v_ref[pl.ds(off, tq), hs(h)],
                                   u, col < row, jnp.zeros((tq, 1), F32))
        acc_ref[h] = contrib
        carry_ref[h] = carry

    def cond(c):
        return (c >= 0) & (jnp.max(carry_ref[...]) > SB_DEAD)

    def body(c):
        off = pl.multiple_of(c * tq, tq)
        for h in range(heads):
            contrib, carry = _sb_block(q_ref[:, hs(h)], k_ref[pl.ds(off, tq), hs(h)],
                                       v_ref[pl.ds(off, tq), hs(h)], u, None, carry_ref[h])
            acc_ref[h] += contrib
            carry_ref[h] = carry
        return c - 1

    lax.while_loop(cond, body, j - 1)
    for h in range(heads):
        o_ref[:, hs(h)] = acc_ref[h]


def _sb_prompt(qb, kvbb, u_strict, *, nb, s, tq, heads):
    nq = s // tq
    m = nb * s
    w = heads * HEAD_DIM
    ng = H_B // heads
    return pl.pallas_call(
        functools.partial(_sb_prompt_kernel, heads=heads),
        grid=(nb, ng, nq),
        in_specs=[pl.BlockSpec((tq, w), lambda b, g, j: (b * nq + j, g)),
                  pl.BlockSpec((s, w), lambda b, g, j: (b, g)),
                  pl.BlockSpec((s, w), lambda b, g, j: (b, ng + g)),
                  pl.BlockSpec((tq, tq), lambda b, g, j: (0, 0))],
        out_specs=pl.BlockSpec((tq, w), lambda b, g, j: (b * nq + j, g)),
        out_shape=jax.ShapeDtypeStruct((m, W_B), F32),
        scratch_shapes=[pltpu.VMEM((heads, tq, HEAD_DIM), F32), pltpu.VMEM((heads, tq, 1), F32)],
        compiler_params=_cparams(3),
        name="sb_prompt",
    )(qb, kvbb, kvbb, u_strict)


def _merge_kernel(oa_ref, ob_ref, ga_ref, gb_ref, wo_ref, x_ref, gpost_ref, gate_ref,
                  gpre_ref, sh_ref, sc_ref, x1_ref, h2_ref):
    na = _rms(oa_ref[...], ga_ref[...]).astype(BF16)
    nb = _rms(ob_ref[...], gb_ref[...]).astype(BF16)
    o = (jnp.dot(na, wo_ref[:W_A, :], preferred_element_type=F32)
         + jnp.dot(nb, wo_ref[W_A:, :], preferred_element_type=F32))
    x1 = x_ref[...] + gate_ref[...] * _rms(o, gpost_ref[...])
    x1_ref[...] = x1
    h2_ref[...] = (_rms(x1, gpre_ref[...]) * (1.0 + sc_ref[...]) + sh_ref[...]).astype(BF16)


def _merge(oa, ob, g_a, g_b, w_out, x2d, g_post, gate, g_pre_ffn, shift, scale, *, tm, rows_per_group):
    m, d = x2d.shape
    bpg = rows_per_group // tm
    r = gate.shape[1]
    row = lambda i: (i, 0)
    grp = lambda i: (i // bpg, 0, 0)
    one = lambda i: (0, 0)
    return pl.pallas_call(
        _merge_kernel,
        grid=(m // tm,),
        in_specs=[pl.BlockSpec((tm, W_A), row), pl.BlockSpec((tm, W_B), row),
                  pl.BlockSpec((1, W_A), one), pl.BlockSpec((1, W_B), one),
                  pl.BlockSpec((W_A + W_B, d), one),
                  pl.BlockSpec((tm, d), row), pl.BlockSpec((1, d), one),
                  pl.BlockSpec((None, r, d), grp), pl.BlockSpec((1, d), one),
                  pl.BlockSpec((None, r, d), grp), pl.BlockSpec((None, r, d), grp)],
        out_specs=[pl.BlockSpec((tm, d), row), pl.BlockSpec((tm, d), row)],
        out_shape=[jax.ShapeDtypeStruct((m, d), F32), jax.ShapeDtypeStruct((m, d), BF16)],
        compiler_params=_cparams(1),
        name="merge_outproj",
    )(oa, ob, g_a, g_b, w_out, x2d, g_post, gate, g_pre_ffn, shift, scale)


def _ffn_kernel(h_ref, wg_ref, wu_ref, wd_ref, x1_ref, gpost_ref, gate_ref, y_ref, act_ref, f_ref, *, nf, nd):
    s = pl.program_id(1)

    @pl.when(s < nf)
    def _():
        h = h_ref[...]
        g = jnp.dot(h, wg_ref[...], preferred_element_type=F32)
        u = jnp.dot(h, wu_ref[...], preferred_element_type=F32)
        act_ref[s] = (g * jax.nn.sigmoid(g) * u).astype(BF16)

    @pl.when(s >= nf)
    def _():
        act = jnp.concatenate([act_ref[f] for f in range(nf)], axis=1)
        f_ref[s - nf] = jnp.dot(act, wd_ref[...], preferred_element_type=F32)

    @pl.when(s == nf + nd - 1)
    def _():
        fo = jnp.concatenate([f_ref[n] for n in range(nd)], axis=1)
        y_ref[...] = x1_ref[...] + gate_ref[...] * _rms(fo, gpost_ref[...])


def _ffn(h2, w_gu, w_down, x1, g_post, gate, *, tm, tf, td, rows_per_group):
    m, d = x1.shape
    ff = w_down.shape[0]
    nf = ff // tf
    nd = d // td
    bpg = rows_per_group // tm
    r = gate.shape[1]
    up_tile = lambda i, s: (0, jnp.minimum(s, nf - 1))
    return pl.pallas_call(
        functools.partial(_ffn_kernel, nf=nf, nd=nd),
        grid=(m // tm, nf + nd),
        in_specs=[pl.BlockSpec((tm, d), lambda i, s: (i, 0)),
                  pl.BlockSpec((d, tf), up_tile),
                  pl.BlockSpec((d, tf), lambda i, s: (0, nf + jnp.minimum(s, nf - 1))),
                  pl.BlockSpec((ff, td), lambda i, s: (0, jnp.maximum(s - nf, 0))),
                  pl.BlockSpec((tm, d), lambda i, s: (i, 0)),
                  pl.BlockSpec((1, d), lambda i, s: (0, 0)),
                  pl.BlockSpec((None, r, d), lambda i, s: (i // bpg, 0, 0))],
        out_specs=pl.BlockSpec((tm, d), lambda i, s: (i, 0)),
        out_shape=jax.ShapeDtypeStruct((m, d), F32),
        scratch_shapes=[pltpu.VMEM((nf, tm, tf), BF16), pltpu.VMEM((nd, tm, td), F32)],
        compiler_params=_cparams(2),
        name="swiglu_ffn",
    )(h2, w_gu, w_gu, w_down, x1, g_post, gate)


def _page_copy(src_ref, buf_ref, sem_ref, pt_ref, b, p, slot):
    return pltpu.make_async_copy(src_ref.at[pt_ref[b, p]], buf_ref.at[slot, p], sem_ref.at[slot])


def _sample_select_kernel(pt_ref, cidx_ref, q_ref, wsel_ref, knew_ref, ptcol_ref, lstrict_ref, ut_ref, eye_ref,
                          idx_ref, k3_ref, pbuf_ref, sem_ref, *, topk, t_new):
    b = pl.program_id(0)
    nb = pl.num_programs(0)
    n_pages = pbuf_ref.shape[1]
    n_rows = k3_ref.shape[1]
    slot = b % 2
    group = 8

    def start_all(bb, sl):
        def body(p, c):
            _page_copy(cidx_ref, pbuf_ref, sem_ref, pt_ref, bb, p, sl).start()
            return c
        lax.fori_loop(0, n_pages, body, 0)

    @pl.when(b == 0)
    def _():
        start_all(0, 0)

    @pl.when(b + 1 < nb)
    def _():
        start_all(b + 1, 1 - slot)

    def wait_body(p, c):
        _page_copy(cidx_ref, pbuf_ref, sem_ref, pt_ref, b, p, slot).wait()
        return c
    lax.fori_loop(0, n_pages, wait_body, 0)

    q = q_ref[...]
    wsel = wsel_ref[...]
    trow = lax.broadcasted_iota(I32, (8, LANES), 0)

    def to_keys(kt):
        dots = jnp.maximum(jnp.dot(q, kt, preferred_element_type=F32), 0.0)
        sc = jnp.dot(wsel, dots.astype(BF16), preferred_element_type=F32)
        return _float_key(sc)

    def score_group(gi, c):
        base = gi * group
        kt = jnp.concatenate([pbuf_ref[slot, base + u] for u in range(group)], axis=1).astype(BF16)
        keys = to_keys(kt)
        for t in range(t_new):
            for u in range(group):
                k3_ref[t, pl.ds(base + u, 1), :] = keys[t:t + 1, u * LANES:(u + 1) * LANES]
        return c
    lax.fori_loop(0, n_pages // group, score_group, 0)

    knew = to_keys(knew_ref[...])
    lane = lax.broadcasted_iota(I32, (8, LANES), 1)
    knew = jnp.where((lane <= trow) & (lane < t_new), knew, INT_MIN)
    pad = jnp.full((n_rows - n_pages - 1, LANES), INT_MIN, I32)
    for t in range(t_new):
        k3_ref[t, n_pages:n_pages + 1, :] = knew[t:t + 1]
        k3_ref[t, n_pages + 1:, :] = pad

    def search_cond(s):
        it = s[0]
        live = s[1] < s[1 + t_new]
        for t in range(1, t_new):
            live = live | (s[1 + t] < s[1 + t_new + t])
        return (it < 40) & live

    def search_body(s):
        it = s[0]
        los, his = [], []
        for t in range(t_new):
            lo, hi = s[1 + t], s[1 + t_new + t]
            mid = _ceil_avg(lo, hi)
            c = jnp.sum(jnp.where(k3_ref[t] >= mid, 1, 0))
            ge = c >= topk
            los.append(jnp.where(ge, mid, lo))
            his.append(jnp.where(c == topk, mid, jnp.where(ge, hi, mid - 1)))
        return (it + 1, *los, *his)

    init = (jnp.int32(0),) + (jnp.int32(INT_MIN + 1),) * t_new + (jnp.int32(INT_MAX),) * t_new
    res = lax.while_loop(search_cond, search_body, init)

    rowid = lax.broadcasted_iota(I32, (n_rows, 1), 0).astype(F32)
    qlane = lax.broadcasted_iota(I32, (1, 2 * LANES), 1).astype(F32)
    liota = lax.broadcasted_iota(I32, (LANES, 2 * LANES), 0).astype(F32)
    lstrict = lstrict_ref[...]
    idx_ref[...] = jnp.zeros(idx_ref.shape, I32)
    past = lax.broadcasted_iota(I32, (n_rows, 1), 0) < n_pages
    kpos = (lax.broadcasted_iota(I32, (n_rows, LANES), 0) * LANES
            + lax.broadcasted_iota(I32, (n_rows, LANES), 1))
    for t in range(t_new):
        thr = res[1 + t]

        n_gt = jnp.sum(jnp.where(k3_ref[t] > thr, 1, 0))

        @pl.when(jnp.sum(jnp.where(k3_ref[t] >= thr, 1, 0)) > topk)
        def _():
            keep = topk - n_gt

            def pos_body(_, s):
                lo, hi = s
                mid = (lo + hi) >> 1
                ok = jnp.sum(jnp.where((k3_ref[t] == thr) & (kpos <= mid), 1, 0)) >= keep
                return jnp.where(ok, lo, mid + 1), jnp.where(ok, mid, hi)

            last, _ = lax.fori_loop(0, (n_rows * LANES - 1).bit_length(), pos_body,
                                    (jnp.int32(0), jnp.int32(n_rows * LANES - 1)))
            kt = k3_ref[t]
            k3_ref[t] = jnp.where((kt == thr) & (kpos > last), thr - 1, kt)

        sel = jnp.where((k3_ref[t] >= thr) & past, 1.0, 0.0)
        selb = sel.astype(BF16)
        cnt = jnp.sum(sel, axis=1, keepdims=True)
        start = jnp.dot(lstrict, jnp.broadcast_to(cnt, (n_rows, LANES)).astype(BF16),
                        preferred_element_type=F32)[:, :1]
        onehot = jnp.where((start <= qlane) & (qlane < start + cnt), 1.0, 0.0)
        nt = (((1,), (1,)), ((), ()))
        et = lax.dot_general(ut_ref[...], selb, nt, preferred_element_type=F32)
        selt = lax.dot_general(eye_ref[...], selb, nt, preferred_element_type=F32)
        ept = jnp.where(selt > 0.5, et, -1.0).astype(BF16)
        gt = jnp.dot(ept, onehot.astype(BF16), preferred_element_type=F32)
        start_q = jnp.sum(onehot * start, axis=0, keepdims=True)
        chunk_q = jnp.sum(onehot * rowid, axis=0, keepdims=True)
        lane_q = jnp.sum(jnp.where(gt == qlane - start_q, liota, 0.0), axis=0, keepdims=True)
        total = jnp.sum(cnt)
        ids = (chunk_q * LANES + lane_q).astype(I32)
        idx_ref[t:t + 1, :] = jnp.where(qlane < total, ids, -1)
        page_q = jnp.sum(onehot * ptcol_ref[...], axis=0, keepdims=True)
        src = ((page_q * PAGE + lane_q) * (2 * HKV_A)).astype(I32)
        idx_ref[8 + t:9 + t, :] = jnp.where(qlane < total, src, 0)
        new_sel = jnp.where(k3_ref[t, n_pages:n_pages + 1, :] >= thr, 1, 0)
        idx_ref[4 + t:5 + t, :] = jnp.concatenate([new_sel, jnp.zeros_like(new_sel)], axis=1)


def _sample_select(page_table, cidx_t, q_s, wsel, knew_t, *, topk, t_new):
    db, n_pages = page_table.shape
    n_rows = ((n_pages + 1 + LANES - 1) // LANES) * LANES
    lstrict = jnp.asarray(np.tril(np.ones((n_rows, n_rows), np.float32), -1), BF16)
    ut = jnp.asarray(np.tril(np.ones((LANES, LANES), np.float32), -1), BF16)
    eye = jnp.asarray(np.eye(LANES, dtype=np.float32), BF16)
    ptcol = jnp.pad(page_table, ((0, 0), (0, n_rows - n_pages))).astype(F32)[:, :, None]
    const = lambda shape: pl.BlockSpec(shape, lambda b, pt: (0,) * len(shape))
    return pl.pallas_call(
        functools.partial(_sample_select_kernel, topk=topk, t_new=t_new),
        grid_spec=pltpu.PrefetchScalarGridSpec(
            num_scalar_prefetch=1, grid=(db,),
            in_specs=[pl.BlockSpec(memory_space=pl.ANY),
                      pl.BlockSpec((None,) + q_s.shape[1:], lambda b, pt: (b, 0, 0)),
                      pl.BlockSpec((None,) + wsel.shape[1:], lambda b, pt: (b, 0, 0)),
                      pl.BlockSpec((None,) + knew_t.shape[1:], lambda b, pt: (b, 0, 0)),
                      pl.BlockSpec((None, n_rows, 1), lambda b, pt: (b, 0, 0)),
                      const((n_rows, n_rows)), const((LANES, LANES)), const((LANES, LANES))],
            out_specs=pl.BlockSpec((None, 16, 2 * LANES), lambda b, pt: (b, 0, 0)),
            scratch_shapes=[pltpu.VMEM((t_new, n_rows, LANES), I32),
                            pltpu.VMEM((2, n_pages, D_IDX, PAGE), F32),
                            pltpu.SemaphoreType.DMA((2,))]),
        out_shape=jax.ShapeDtypeStruct((db, 16, 2 * LANES), I32),
        compiler_params=_cparams(1),
        name="sample_select",
    )(page_table, cidx_t, q_s, wsel, knew_t, ptcol, lstrict, ut, eye)


def _sample_attend_kernel(src_s_ref, rb_ref, kva_ref, kvnew_ref, idxv_ref, q_ref, o_ref,
                          gbuf_ref, nbuf_ref, sem_ref, *, t_new, n_past):
    b = pl.program_id(0)
    nsel = idxv_ref.shape[1]
    rows = 2 * HKV_A
    unroll = 8

    new_rows = t_new * rows
    cp_new = pltpu.make_async_copy(kvnew_ref.at[pl.ds(b * new_rows, new_rows)],
                                   nbuf_ref.at[pl.ds(0, new_rows)], sem_ref.at[1])
    cp_new.start()
    nbuf_ref[pl.ds(new_rows, nbuf_ref.shape[0] - new_rows), :] = jnp.zeros(
        (nbuf_ref.shape[0] - new_rows, HEAD_DIM), F32)

    def row_copy(t, qi):
        src = src_s_ref[b * t_new + t, qi]
        return pltpu.make_async_copy(kva_ref.at[pl.ds(pl.multiple_of(src, rows), rows)],
                                     gbuf_ref.at[t, pl.ds(pl.multiple_of(qi * rows, rows), rows)],
                                     sem_ref.at[0])

    for t in range(t_new):
        def issue(qo, c):
            for u in range(unroll):
                row_copy(t, qo * unroll + u).start()
            return c
        lax.fori_loop(0, nsel // unroll, issue, 0)
    for t in range(t_new):
        def drain(qo, c):
            for u in range(unroll):
                row_copy(t, qo * unroll + u).wait()
            return c
        lax.fori_loop(0, nsel // unroll, drain, 0)
    cp_new.wait()

    nlane = lax.broadcasted_iota(I32, (1, LANES), 1)
    zpad = jnp.zeros((LANES - 8, HEAD_DIM), BF16)
    nt = (((1,), (1,)), ((), ()))
    for t in range(t_new):
        ids = idxv_ref[t:t + 1, :]
        valid = ids >= 0
        dist = (n_past + t) - ids
        new_valid = (idxv_ref[4 + t:5 + t, :LANES] > 0) & (nlane < t_new)
        new_dist = t - nlane
        for g in range(HKV_A):
            k = gbuf_ref[t, pl.ds(g, nsel, stride=rows), :].astype(BF16)
            v = gbuf_ref[t, pl.ds(HKV_A + g, nsel, stride=rows), :].astype(BF16)
            kn = jnp.concatenate([nbuf_ref[pl.ds(g, 8, stride=rows), :].astype(BF16), zpad], axis=0)
            vn = jnp.concatenate([nbuf_ref[pl.ds(HKV_A + g, 8, stride=rows), :].astype(BF16), zpad], axis=0)
            qg = q_ref[t, g]
            srow = lax.broadcasted_iota(I32, (8, 1), 0)
            s = lax.dot_general(qg, k, nt, preferred_element_type=F32) * SCALE
            s = s + jnp.where(srow == 0, _rel_bias_of(dist, rb_ref, 2 * g), _rel_bias_of(dist, rb_ref, 2 * g + 1))
            s = jnp.where(valid, s, -jnp.inf)
            sn = lax.dot_general(qg, kn, nt, preferred_element_type=F32) * SCALE
            sn = sn + jnp.where(srow == 0, _rel_bias_of(new_dist, rb_ref, 2 * g),
                                _rel_bias_of(new_dist, rb_ref, 2 * g + 1))
            sn = jnp.where(new_valid, sn, -jnp.inf)
            mx = jnp.maximum(jnp.max(s, axis=1, keepdims=True), jnp.max(sn, axis=1, keepdims=True))
            p = jnp.exp(s - mx)
            pn = jnp.exp(sn - mx)
            den = jnp.sum(p, axis=1, keepdims=True) + jnp.sum(pn, axis=1, keepdims=True)
            p = p / den
            pn = pn / den
            out = (jnp.dot(p.astype(BF16), v, preferred_element_type=F32)
                   + jnp.dot(pn.astype(BF16), vn, preferred_element_type=F32))
            o_ref[t:t + 1, (2 * g) * HEAD_DIM:(2 * g + 1) * HEAD_DIM] = out[0:1]
            o_ref[t:t + 1, (2 * g + 1) * HEAD_DIM:(2 * g + 2) * HEAD_DIM] = out[1:2]
    o_ref[t_new:, :] = jnp.zeros((8 - t_new, W_A), F32)


def _sample_attend(idx, rel_bias, kva_rows, kvnew_rows, q_pad, *, t_new, n_past):
    db, _, nsel = idx.shape
    src_s = idx[:, 8:8 + t_new].reshape(db * t_new, nsel)
    return pl.pallas_call(
        functools.partial(_sample_attend_kernel, t_new=t_new, n_past=n_past),
        grid_spec=pltpu.PrefetchScalarGridSpec(
            num_scalar_prefetch=2, grid=(db,),
            in_specs=[pl.BlockSpec(memory_space=pl.ANY), pl.BlockSpec(memory_space=pl.ANY),
                      pl.BlockSpec((None, 16, nsel), lambda b, *_: (b, 0, 0)),
                      pl.BlockSpec((None,) + q_pad.shape[1:], lambda b, *_: (b, 0, 0, 0, 0))],
            out_specs=pl.BlockSpec((None, 8, W_A), lambda b, *_: (b, 0, 0)),
            scratch_shapes=[pltpu.VMEM((t_new, nsel * 2 * HKV_A, HEAD_DIM), F32),
                            pltpu.VMEM((8 * 2 * HKV_A, HEAD_DIM), F32),
                            pltpu.SemaphoreType.DMA((2,))]),
        out_shape=jax.ShapeDtypeStruct((db, 8, W_A), F32),
        compiler_params=_cparams(1),
        name="sample_attend",
    )(src_s, rel_bias, kva_rows, kvnew_rows, idx, q_pad)


def _sb_sample_kernel(pt_ref, kvb_ref, kvnew_ref, q_ref, u_ref, o_ref, pbuf_ref, acc_ref, carry_ref,
                      sem_ref, *, t_new):
    b = pl.program_id(0)
    n_pages = pt_ref.shape[1]
    prow = pbuf_ref.shape[1]
    step = 2 * H_B
    u = u_ref[...]
    trow = lax.broadcasted_iota(I32, (8, PAGE), 0)
    scol = lax.broadcasted_iota(I32, (8, PAGE), 1)
    live_rows = lax.broadcasted_iota(I32, (8, 1), 0) < t_new

    def page_copy(p, slot):
        return pltpu.make_async_copy(kvb_ref.at[pl.ds(pt_ref[b, p] * prow, prow)], pbuf_ref.at[slot],
                                     sem_ref.at[slot])

    def process(slot, mask):
        for h in range(H_B):
            k = pbuf_ref[slot, pl.ds(h, PAGE, stride=step), :].astype(BF16)
            v = pbuf_ref[slot, pl.ds(H_B + h, PAGE, stride=step), :].astype(BF16)
            contrib, carry = _sb_block(q_ref[h], k, v, u, mask, carry_ref[h])
            acc_ref[h] += contrib
            carry_ref[h] = carry

    page_copy(n_pages - 1, 0).start()
    new_rows = t_new * step
    cp_new = pltpu.make_async_copy(kvnew_ref.at[pl.ds(b * new_rows, new_rows)],
                                   pbuf_ref.at[1, pl.ds(0, new_rows)], sem_ref.at[1])
    cp_new.start()
    pbuf_ref[1, pl.ds(new_rows, prow - new_rows), :] = jnp.zeros((prow - new_rows, HEAD_DIM), F32)
    acc_ref[...] = jnp.zeros(acc_ref.shape, F32)
    carry_ref[...] = jnp.zeros(carry_ref.shape, F32)
    cp_new.wait()
    process(1, (scol < trow) & (scol < t_new))

    def alive():
        c = jnp.where(live_rows, carry_ref[...], -jnp.inf)
        return jnp.max(c) > SB_DEAD

    def cond(s):
        i, go = s
        return (i < n_pages) & go

    def body(s):
        i, _ = s
        slot = i % 2
        page_copy(n_pages - 1 - i, slot).wait()

        @pl.when(i + 1 < n_pages)
        def _():
            page_copy(n_pages - 2 - i, 1 - slot).start()
        process(slot, None)
        return i + 1, alive()

    i_end, _ = lax.while_loop(cond, body, (jnp.int32(0), alive()))

    @pl.when(i_end < n_pages)
    def _():
        page_copy(n_pages - 1 - i_end, i_end % 2).wait()

    for h in range(H_B):
        o_ref[:, h * HEAD_DIM:(h + 1) * HEAD_DIM] = acc_ref[h]


def _sb_sample(page_table, kvb_rows, kvnew_rows, q_pad, u_strict, *, t_new):
    db = page_table.shape[0]
    prow = PAGE * 2 * H_B
    return pl.pallas_call(
        functools.partial(_sb_sample_kernel, t_new=t_new),
        grid_spec=pltpu.PrefetchScalarGridSpec(
            num_scalar_prefetch=1, grid=(db,),
            in_specs=[pl.BlockSpec(memory_space=pl.ANY), pl.BlockSpec(memory_space=pl.ANY),
                      pl.BlockSpec((None, H_B, 8, HEAD_DIM), lambda b, pt: (b, 0, 0, 0)),
                      pl.BlockSpec((PAGE, PAGE), lambda b, pt: (0, 0))],
            out_specs=pl.BlockSpec((None, 8, W_B), lambda b, pt: (b, 0, 0)),
            scratch_shapes=[pltpu.VMEM((2, prow, HEAD_DIM), F32),
                            pltpu.VMEM((H_B, 8, HEAD_DIM), F32),
                            pltpu.VMEM((H_B, 8, 1), F32),
                            pltpu.SemaphoreType.DMA((2,))]),
        out_shape=jax.ShapeDtypeStruct((db, 8, W_B), F32),
        compiler_params=_cparams(1),
        name="sb_sample",
    )(page_table, kvb_rows, kvnew_rows, q_pad, u_strict)


def _strict_upper(n):
    return jnp.asarray(np.tril(np.ones((n, n), np.float32), -1), BF16)


def kernel(x_prompt, x_sample, cache_kv_a, cache_idx, cache_kv_b, page_table, c_prompt, c_sample, rel_bias, w_ada, b_ada, g_pre_mix, w_in, g_grp_a, g_grp_b, w_out, g_post_mix, g_pre_ffn, w_gate_up, w_down, g_post_ffn):
    nb, s, d = x_prompt.shape
    db, t_new, _ = x_sample.shape
    n_pages = page_table.shape[1]
    n_past = n_pages * PAGE
    assert w_ada.shape[0] == 1 and d == (H_A + H_B) * HEAD_DIM
    assert cache_kv_a.shape[2:] == (PAGE, 2, HKV_A, HEAD_DIM) and t_new <= 4

    sp = np.cumsum([0, W_A, KV_A, KV_A, H_IDX * D_IDX, D_IDX, H_IDX, W_B, W_B, W_B])
    wi = w_in[0]
    col = lambda i: wi[:, sp[i]:sp[i + 1]]
    w_main = jnp.concatenate([col(0), col(1), col(2), col(3), col(6), col(7), col(8)], axis=1).astype(BF16)
    w_small = jnp.concatenate([col(4), col(5), jnp.zeros((d, LANES - D_IDX - H_IDX), F32)], axis=1).astype(BF16)
    w_small_t = w_small.T
    w_out_b = w_out[0].astype(BF16)
    w_gu_b = w_gate_up[0].astype(BF16)
    w_down_b = w_down[0].astype(BF16)

    mod = _modulation(jnp.concatenate([c_prompt, c_sample], axis=0), w_ada[0], b_ada)
    mod6 = mod.reshape(nb + db, 6, d).transpose(1, 0, 2)
    mp = [mod6[i, :nb].reshape(nb, 1, d) for i in range(6)]
    ms = [jnp.repeat(mod6[i, nb:], t_new, axis=0)[None] for i in range(6)]

    tq = 256
    topk_p = min(TOPK_MAX, s // 4)
    xp = x_prompt.reshape(nb * s, d)
    (qa, kva, kvab, qi, qb, kvb, kvbb, sm, kt) = _project(
        xp, g_pre_mix, mp[0], mp[1], w_main, w_small, w_small_t, tm=min(512, s), rows_per_group=s)
    bias_tiles = _bias_tiles(rel_bias, tq)
    oa = _dsa_prompt(rel_bias, qi, sm, qa, kvab, bias_tiles, nb=nb, s=s, tq=tq, topk=topk_p)
    ob = _sb_prompt(qb, kvbb, _strict_upper(tq), nb=nb, s=s, tq=tq, heads=4)
    x1, h2 = _merge(oa, ob, g_grp_a, g_grp_b, w_out_b, xp, g_post_mix, mp[2], g_pre_ffn, mp[3], mp[4],
                    tm=256, rows_per_group=s)
    y_prompt = _ffn(h2, w_gu_b, w_down_b, x1, g_post_ffn, mp[5], tm=512, tf=512, td=512, rows_per_group=s)

    ms_rows = db * t_new
    topk_s = min(TOPK_MAX, (n_past + t_new) // 4)
    xs = x_sample.reshape(ms_rows, d)
    (qa_s, kva_s, _, qi_s, qb_s, kvb_s, _, sm_s, kt_s) = _project(
        xs, g_pre_mix, ms[0], ms[1], w_main, w_small, w_small_t, tm=ms_rows, rows_per_group=ms_rows)

    q_s = qi_s.reshape(db, t_new * H_IDX, D_IDX)
    w_s = sm_s[:, D_IDX:D_IDX + H_IDX].reshape(db, t_new, H_IDX)
    wsel = (jnp.eye(8, t_new, dtype=F32)[None, :, :, None] * w_s[:, None]).reshape(db, 8, t_new * H_IDX)
    knew_t = jnp.pad(kt_s[0].reshape(D_IDX, db, t_new).transpose(1, 0, 2),
                     ((0, 0), (0, 0), (0, LANES - t_new))).astype(BF16)
    cidx_t = jnp.swapaxes(cache_idx[0], 1, 2)
    idx = _sample_select(page_table, cidx_t, q_s, wsel.astype(BF16), knew_t, topk=topk_s, t_new=t_new)

    kva_rows = cache_kv_a.reshape(-1, HEAD_DIM)
    kvnew_rows = kva_s
    qa4 = qa_s.reshape(db, t_new, HKV_A, H_A // HKV_A, HEAD_DIM)
    q_pad = jnp.pad(qa4, ((0, 0), (0, 0), (0, 0), (0, 8 - H_A // HKV_A), (0, 0)))
    oa_s = _sample_attend(idx, rel_bias, kva_rows, kvnew_rows, q_pad, t_new=t_new, n_past=n_past)

    kvb_rows = cache_kv_b.reshape(-1, HEAD_DIM)
    kvbnew_rows = kvb_s
    qb_pad = jnp.pad(qb_s.reshape(db, t_new, H_B, HEAD_DIM).transpose(0, 2, 1, 3),
                     ((0, 0), (0, 0), (0, 8 - t_new), (0, 0)))
    ob_s = _sb_sample(page_table, kvb_rows, kvbnew_rows, qb_pad, _strict_upper(PAGE), t_new=t_new)

    oa_s = oa_s[:, :t_new].reshape(ms_rows, W_A)
    ob_s = ob_s[:, :t_new].reshape(ms_rows, W_B)
    x1_s, h2_s = _merge(oa_s, ob_s, g_grp_a, g_grp_b, w_out_b, xs, g_post_mix, ms[2], g_pre_ffn, ms[3],
                        ms[4], tm=ms_rows, rows_per_group=ms_rows)
    y_sample = _ffn(h2_s, w_gu_b, w_down_b, x1_s, g_post_ffn, ms[5], tm=ms_rows, tf=512, td=512,
                    rows_per_group=ms_rows)

    return (y_prompt.reshape(nb, s, d),
            y_sample.reshape(db, t_new, d),
            kva.reshape(1, nb, s, 2, HKV_A, HEAD_DIM),
            jnp.swapaxes(kt, 1, 2)[None],
            kvb.reshape(1, nb, s, 2, H_B, HEAD_DIM),
            kva_s.reshape(1, db, t_new, 2, HKV_A, HEAD_DIM),
            sm_s[:, :D_IDX].reshape(1, db, t_new, D_IDX),
            kvb_s.reshape(1, db, t_new, 2, H_B, HEAD_DIM))
```

```python
import functools
import math

import jax
import jax.numpy as jnp
import numpy as np
from jax import lax
from jax.experimental import pallas as pl
from jax.experimental.pallas import tpu as pltpu

F32 = jnp.float32
BF16 = jnp.bfloat16
I32 = jnp.int32

HEAD_DIM = 128
H_A = 8
HKV_A = 4
H_B = 8
H_IDX = 16
D_IDX = 64
TOPK_MAX = 256
N_BUCKETS = 32
MAX_DISTANCE = 128
PAGE = 128
EPS = 1e-6
SCALE = HEAD_DIM ** -0.5
IDX_W_SCALE = (H_IDX * D_IDX) ** -0.5
W_A = H_A * HEAD_DIM
KV_A = HKV_A * HEAD_DIM
W_B = H_B * HEAD_DIM
INT_MIN = -2 ** 31
INT_MAX = 2 ** 31 - 1
SB_DEAD = -120.0

LANES = 128
VMEM_LIMIT = 56 * 1024 * 1024


def _cparams(n_axes, vmem=VMEM_LIMIT):
    return pltpu.CompilerParams(dimension_semantics=("arbitrary",) * n_axes,
                                vmem_limit_bytes=vmem)


def _bucket_thresholds():
    max_exact = N_BUCKETS // 2
    d = np.arange(0, 4 * MAX_DISTANCE, dtype=np.int32)
    df = np.maximum(d, 1).astype(np.float32)
    large = max_exact + (np.log(df / np.float32(max_exact)) / np.float32(math.log(MAX_DISTANCE / max_exact))
                         * np.float32(N_BUCKETS - max_exact)).astype(np.int32)
    large = np.minimum(large, N_BUCKETS - 1)
    bucket = np.where(d < max_exact, d, large)
    assert bucket[-1] == N_BUCKETS - 1 and np.all(np.diff(bucket) >= 0)
    return [int(np.argmax(bucket >= b)) for b in range(1, N_BUCKETS)]


_BUCKET_THR = _bucket_thresholds()


def _rel_bias_of(dist, rb_ref, h):
    val = jnp.full(dist.shape, rb_ref[0, h], F32)
    for b in range(1, N_BUCKETS):
        val = jnp.where(dist >= _BUCKET_THR[b - 1], rb_ref[b, h], val)
    return val


def _float_key(x):
    bits = lax.bitcast_convert_type(x, I32)
    return bits ^ ((bits >> 31) & INT_MAX)


def _ceil_avg(lo, hi):
    x = lo ^ hi
    return (lo & hi) + (x >> 1) + (x & 1)


def _rms(x, g):
    return x * lax.rsqrt(jnp.mean(x * x, axis=-1, keepdims=True) + EPS) * g


def _mod_kernel(c_ref, w_ref, b_ref, o_ref):
    c = c_ref[...]
    a = c * jax.nn.sigmoid(c)
    o_ref[...] = jnp.dot(a, w_ref[...], preferred_element_type=F32,
                         precision=lax.Precision.HIGHEST) + b_ref[...]


def _modulation(c_all, w_ada, b_ada):
    r, d = c_all.shape
    n = w_ada.shape[1]
    tn = 1024
    return pl.pallas_call(
        _mod_kernel,
        grid=(n // tn,),
        in_specs=[pl.BlockSpec((r, d), lambda i: (0, 0)),
                  pl.BlockSpec((d, tn), lambda i: (0, i)),
                  pl.BlockSpec((1, tn), lambda i: (0, i))],
        out_specs=pl.BlockSpec((r, tn), lambda i: (0, i)),
        out_shape=jax.ShapeDtypeStruct((r, n), F32),
        compiler_params=_cparams(1),
        name="adaln_mod",
    )(c_all, w_ada, b_ada)


PROJ_TN = 1024
_PROJ_GROUPS = {"qa": (0, 1), "kva": (1, 1), "qi": (2, 1), "qb": (3, 1), "kvb": (4, 2)}
N_MAIN = 6


def _proj_kernel(x_ref, g_ref, sh_ref, sc_ref, wm_ref, ws_ref, wst_ref,
                 qa_ref, kva_ref, kvab_ref, qi_ref, qb_ref, kvb_ref, kvbb_ref,
                 sm_ref, kt_ref, h_ref):
    n = pl.program_id(1)

    @pl.when(n == 0)
    def _():
        y = _rms(x_ref[...], g_ref[...])
        h_ref[...] = (y * (1.0 + sc_ref[...]) + sh_ref[...]).astype(BF16)

    def mm():
        return jnp.dot(h_ref[...], wm_ref[...], preferred_element_type=F32)

    def in_group(name):
        first, count = _PROJ_GROUPS[name]
        return (n >= first) & (n < first + count)

    @pl.when(in_group("qa"))
    def _():
        qa_ref[...] = mm().astype(BF16)

    def store_heads(out_ref, r, first_row, rows_per_token):
        for j in range(r.shape[1] // HEAD_DIM):
            out_ref[pl.ds(first_row + j, r.shape[0], stride=rows_per_token), :] = r[:, j * HEAD_DIM:(j + 1) * HEAD_DIM]

    @pl.when(in_group("kva"))
    def _():
        r = mm()
        store_heads(kva_ref, r, 0, 2 * HKV_A)
        kvab_ref[...] = r.astype(BF16)

    @pl.when(in_group("qi"))
    def _():
        qi_ref[...] = mm().astype(BF16)

    @pl.when(in_group("qb"))
    def _():
        qb_ref[...] = mm().astype(BF16)

    for part in range(_PROJ_GROUPS["kvb"][1]):
        @pl.when(n == _PROJ_GROUPS["kvb"][0] + part)
        def _():
            r = mm()
            store_heads(kvb_ref, r, part * H_B, 2 * H_B)
            kvbb_ref[...] = r.astype(BF16)

    @pl.when(n == N_MAIN)
    def _():
        r = jnp.dot(h_ref[...], ws_ref[...], preferred_element_type=F32)
        lane = lax.broadcasted_iota(I32, r.shape, 1)
        is_w = (lane >= D_IDX) & (lane < D_IDX + H_IDX)
        sm_ref[...] = jnp.where(is_w, r * IDX_W_SCALE, r)
        rt = lax.dot_general(wst_ref[...], h_ref[...], (((1,), (1,)), ((), ())),
                             preferred_element_type=F32)
        kt_ref[...] = rt[:D_IDX]


def _project(x2d, g_pre, shift, scale, w_main, w_small, w_small_t, *, tm, rows_per_group):
    m, d = x2d.shape
    nb = m // tm
    bpg = rows_per_group // tm
    r = shift.shape[1]
    ng = m // rows_per_group
    tn = PROJ_TN
    grid = (nb, N_MAIN + 1)
    row = lambda i, n: (i, 0)
    grp = lambda i, n: (i // bpg, 0, 0)

    def tile(name):
        first, count = _PROJ_GROUPS[name]
        return pl.BlockSpec((tm, tn), lambda i, n: (i, jnp.clip(n - first, 0, count - 1)))

    out_shape = [
        jax.ShapeDtypeStruct((m, W_A), BF16),
        jax.ShapeDtypeStruct((m * 2 * HKV_A, HEAD_DIM), F32),
        jax.ShapeDtypeStruct((m, 2 * KV_A), BF16),
        jax.ShapeDtypeStruct((m, H_IDX * D_IDX), BF16),
        jax.ShapeDtypeStruct((m, W_B), BF16),
        jax.ShapeDtypeStruct((m * 2 * H_B, HEAD_DIM), F32),
        jax.ShapeDtypeStruct((m, 2 * W_B), BF16),
        jax.ShapeDtypeStruct((m, LANES), F32),
        jax.ShapeDtypeStruct((ng, D_IDX, rows_per_group), F32),
    ]
    out_specs = [
        tile("qa"), pl.BlockSpec((tm * 2 * HKV_A, HEAD_DIM), row), tile("kva"), tile("qi"), tile("qb"),
        pl.BlockSpec((tm * 2 * H_B, HEAD_DIM), row), tile("kvb"),
        pl.BlockSpec((tm, LANES), row),
        pl.BlockSpec((None, D_IDX, tm), lambda i, n: (i // bpg, 0, i % bpg)),
    ]
    in_specs = [
        pl.BlockSpec((tm, d), row),
        pl.BlockSpec((1, d), lambda i, n: (0, 0)),
        pl.BlockSpec((None, r, d), grp),
        pl.BlockSpec((None, r, d), grp),
        pl.BlockSpec((d, tn), lambda i, n: (0, jnp.minimum(n, N_MAIN - 1))),
        pl.BlockSpec((d, LANES), lambda i, n: (0, 0)),
        pl.BlockSpec((LANES, d), lambda i, n: (0, 0)),
    ]
    return pl.pallas_call(
        _proj_kernel,
        grid=grid, in_specs=in_specs, out_specs=out_specs, out_shape=out_shape,
        scratch_shapes=[pltpu.VMEM((tm, d), BF16)],
        compiler_params=_cparams(2),
        name="prenorm_proj",
    )(x2d, g_pre, shift, scale, w_main, w_small, w_small_t)


def _bias_tiles_kernel(rb_ref, o_ref):
    h = pl.program_id(0)
    tq = o_ref.shape[-1]
    key = lax.broadcasted_iota(I32, (tq, tq), 0)
    qry = lax.broadcasted_iota(I32, (tq, tq), 1)
    for k in range(2):
        o_ref[k] = _rel_bias_of(qry - key + k * tq, rb_ref, h)


def _bias_tiles(rel_bias, tq):
    return pl.pallas_call(
        _bias_tiles_kernel,
        grid=(H_A,),
        in_specs=[pl.BlockSpec(memory_space=pltpu.SMEM)],
        out_specs=pl.BlockSpec((2, None, tq, tq), lambda h: (0, h, 0, 0)),
        out_shape=jax.ShapeDtypeStruct((2, H_A, tq, tq), F32),
        compiler_params=_cparams(1),
        name="t5_bias_tiles",
    )(rel_bias)


GROUPS_PER_PASS = 4


def _dsa_prompt_kernel(rb_ref, qi_ref, smq_ref, smk_ref, qa_ref, kv_ref, bias_ref,
                       o_ref, keys_ref, kab_ref, vt_ref, qit_ref, qat_ref, m_ref, l_ref, acc_ref, *, topk):
    j = pl.program_id(1)
    tq = qa_ref.shape[0]
    nq = keys_ref.shape[0]
    nkc = j + 1
    qpos = j * tq + lax.broadcasted_iota(I32, (1, tq), 1)

    @pl.when(j == 0)
    def _():
        def prep(c, carry):
            off = pl.multiple_of(c * tq, tq)
            r = smk_ref[pl.ds(off, tq), :]
            lane = lax.broadcasted_iota(I32, r.shape, 1)
            ka = jnp.where(lane < D_IDX, r, 0.0)
            kb = jnp.where(lane >= D_IDX, pltpu.roll(r, D_IDX, axis=1), 0.0)
            kab_ref[c] = jnp.concatenate([ka, kb], axis=0).astype(BF16)
            for g in range(HKV_A):
                v = kv_ref[pl.ds(off, tq), KV_A + g * HEAD_DIM:KV_A + (g + 1) * HEAD_DIM]
                vt_ref[c, g] = v.astype(F32).T.astype(BF16)
            return carry

        lax.fori_loop(0, nq, prep, 0)

    for hp in range(H_IDX // 2):
        qit_ref[hp] = qi_ref[:, hp * LANES:(hp + 1) * LANES].astype(F32).T.astype(BF16)
    for h in range(H_A):
        qat_ref[h] = qa_ref[:, h * HEAD_DIM:(h + 1) * HEAD_DIM].astype(F32).T.astype(BF16)
    wt = smq_ref[...].T[D_IDX:D_IDX + H_IDX]

    def score_chunk(c, carry):
        kab = kab_ref[c]
        acc = jnp.zeros((tq, tq), F32)
        for hp in range(H_IDX // 2):
            d = jnp.maximum(jnp.dot(kab, qit_ref[hp], preferred_element_type=F32), 0.0)
            acc = acc + wt[2 * hp:2 * hp + 1] * d[:tq] + wt[2 * hp + 1:2 * hp + 2] * d[tq:]
        kpos = c * tq + lax.broadcasted_iota(I32, (tq, 1), 0)
        keys_ref[c] = jnp.where(kpos <= qpos, _float_key(acc), INT_MIN)
        return carry

    lax.fori_loop(0, nkc, score_chunk, 0)

    need = (qpos + 1) > topk

    def count_ge(mid):
        def body(c, cnt):
            hit = jnp.where(keys_ref[c] >= mid, 1, 0)
            return cnt + jnp.sum(hit.reshape(tq // 8, 8, tq), axis=0)

        cnt = lax.fori_loop(0, nkc, body, jnp.zeros((8, tq), I32))
        return jnp.sum(cnt, axis=0, keepdims=True)

    def search_body(_, s):
        lo, hi = s
        mid = _ceil_avg(lo, hi)
        c = count_ge(mid)
        ge = c >= topk
        return jnp.where(ge, mid, lo), jnp.where(c == topk, mid, jnp.where(ge, hi, mid - 1))

    lo0 = jnp.full((1, tq), INT_MIN + 1, I32)
    hi0 = jnp.where(need, INT_MAX, INT_MIN + 1)
    thr, _ = lax.fori_loop(0, 32, search_body, (lo0, hi0))

    n_gt = count_ge(thr + 1)
    excess = count_ge(thr) > topk

    @pl.when(jnp.max(jnp.where(excess, 1.0, 0.0)) > 0.5)
    def _():
        keep = topk - n_gt

        def tied_upto(pos):
            def body(c, cnt):
                kpos = c * tq + lax.broadcasted_iota(I32, (tq, 1), 0)
                hit = jnp.where((keys_ref[c] == thr) & (kpos <= pos), 1, 0)
                return cnt + jnp.sum(hit.reshape(tq // 8, 8, tq), axis=0)

            cnt = lax.fori_loop(0, nkc, body, jnp.zeros((8, tq), I32))
            return jnp.sum(cnt, axis=0, keepdims=True)

        def pos_body(_, s):
            lo, hi = s
            mid = (lo + hi) >> 1
            ok = tied_upto(mid) >= keep
            return jnp.where(ok, lo, mid + 1), jnp.where(ok, mid, hi)

        last, _ = lax.fori_loop(0, (nq * tq - 1).bit_length(), pos_body,
                                (jnp.zeros((1, tq), I32), jnp.full((1, tq), nq * tq - 1, I32)))

        def demote(c, carry):
            kc = keys_ref[c]
            kpos = c * tq + lax.broadcasted_iota(I32, (tq, 1), 0)
            keys_ref[c] = jnp.where(excess & (kc == thr) & (kpos > last), thr - 1, kc)
            return carry

        lax.fori_loop(0, nkc, demote, 0)

    far = N_BUCKETS - 1
    left = lax.broadcasted_iota(I32, (1, 2 * tq), 1) < tq
    for g0 in range(0, HKV_A, GROUPS_PER_PASS):
        groups = range(g0, g0 + GROUPS_PER_PASS)
        qt2 = [jnp.concatenate([qat_ref[2 * g], qat_ref[2 * g + 1]], axis=1) for g in groups]
        m_ref[...] = jnp.full(m_ref.shape, -jnp.inf, F32)
        l_ref[...] = jnp.zeros(l_ref.shape, F32)
        acc_ref[...] = jnp.zeros(acc_ref.shape, F32)

        def block(c, bias_of):
            off = pl.multiple_of(c * tq, tq)
            sel = keys_ref[c] >= thr
            sel2 = jnp.concatenate([sel, sel], axis=1)
            for i, g in enumerate(groups):
                k = kv_ref[pl.ds(off, tq), g * HEAD_DIM:(g + 1) * HEAD_DIM]
                s = jnp.dot(k, qt2[i], preferred_element_type=F32) * SCALE + bias_of(g)
                s = jnp.where(sel2, s, -jnp.inf)
                m_old = m_ref[i]
                m_new = jnp.maximum(m_old, jnp.max(s, axis=0, keepdims=True))
                m_safe = jnp.where(m_new == -jnp.inf, 0.0, m_new)
                p = jnp.exp(s - m_safe)
                alpha = jnp.exp(m_old - m_safe)
                l_ref[i] = alpha * l_ref[i] + jnp.sum(p, axis=0, keepdims=True)
                acc_ref[i] = alpha * acc_ref[i] + jnp.dot(vt_ref[c, g], p.astype(BF16),
                                                          preferred_element_type=F32)
                m_ref[i] = m_new

        far_bias = {g: jnp.where(left, rb_ref[far, 2 * g], rb_ref[far, 2 * g + 1]) for g in groups}

        def far_block(c, carry):
            block(c, lambda g: far_bias[g])
            return carry

        lax.fori_loop(0, j - 1, far_block, 0)

        def near_bias(k):
            return lambda g: jnp.concatenate([bias_ref[k, 2 * g], bias_ref[k, 2 * g + 1]], axis=1)

        @pl.when(j >= 1)
        def _():
            block(j - 1, near_bias(1))

        block(j, near_bias(0))

        for i, g in enumerate(groups):
            out = acc_ref[i] / l_ref[i]
            o_ref[:, (2 * g) * HEAD_DIM:(2 * g + 1) * HEAD_DIM] = out[:, :tq].T
            o_ref[:, (2 * g + 1) * HEAD_DIM:(2 * g + 2) * HEAD_DIM] = out[:, tq:].T


def _dsa_prompt(rel_bias, qi, sm, qa, kvab, bias_tiles, *, nb, s, tq, topk):
    nq = s // tq
    m = nb * s
    qrow = lambda b, j, rb: (b * nq + j, 0)
    in_specs = [
        pl.BlockSpec((tq, H_IDX * D_IDX), qrow),
        pl.BlockSpec((tq, LANES), qrow),
        pl.BlockSpec((s, LANES), lambda b, j, rb: (b, 0)),
        pl.BlockSpec((tq, W_A), qrow),
        pl.BlockSpec((s, 2 * KV_A), lambda b, j, rb: (b, 0)),
        pl.BlockSpec((2, H_A, tq, tq), lambda b, j, rb: (0, 0, 0, 0)),
    ]
    return pl.pallas_call(
        functools.partial(_dsa_prompt_kernel, topk=topk),
        grid_spec=pltpu.PrefetchScalarGridSpec(
            num_scalar_prefetch=1, grid=(nb, nq), in_specs=in_specs,
            out_specs=pl.BlockSpec((tq, W_A), qrow),
            scratch_shapes=[pltpu.VMEM((nq, tq, tq), I32),
                            pltpu.VMEM((nq, 2 * tq, LANES), BF16),
                            pltpu.VMEM((nq, HKV_A, HEAD_DIM, tq), BF16),
                            pltpu.VMEM((H_IDX // 2, LANES, tq), BF16),
                            pltpu.VMEM((H_A, HEAD_DIM, tq), BF16),
                            pltpu.VMEM((GROUPS_PER_PASS, 1, 2 * tq), F32),
                            pltpu.VMEM((GROUPS_PER_PASS, 1, 2 * tq), F32),
                            pltpu.VMEM((GROUPS_PER_PASS, HEAD_DIM, 2 * tq), F32)]),
        out_shape=jax.ShapeDtypeStruct((m, W_A), F32),
        compiler_params=_cparams(2),
        name="dsa_prompt",
    )(rel_bias, qi, sm, sm, qa, kvab, bias_tiles)


def _sb_block(q, k, v, u_strict, mask, carry):
    z = lax.dot_general(q, k, (((1,), (1,)), ((), ())), preferred_element_type=F32) * SCALE
    l1p = jnp.log1p(jnp.exp(-jnp.abs(z)))
    ls_pos = jnp.minimum(z, 0.0) - l1p
    ls_neg = jnp.minimum(-z, 0.0) - l1p
    lb1 = ls_neg if mask is None else jnp.where(mask, ls_neg, 0.0)
    hi = lb1.astype(BF16)
    lo = (lb1 - hi.astype(F32)).astype(BF16)
    suffix = (jnp.dot(hi, u_strict, preferred_element_type=F32)
              + jnp.dot(lo, u_strict, preferred_element_type=F32)) + carry
    a = jnp.exp(ls_pos + suffix)
    if mask is not None:
        a = jnp.where(mask, a, 0.0)
    contrib = jnp.dot(a.astype(BF16), v, preferred_element_type=F32)
    return contrib, carry + jnp.sum(lb1, axis=1, keepdims=True)


def _sb_prompt_kernel(q_ref, k_ref, v_ref, u_ref, o_ref, acc_ref, carry_ref, *, heads):
    j = pl.program_id(2)
    tq = q_ref.shape[0]
    u = u_ref[...]
    row = lax.broadcasted_iota(I32, (tq, tq), 0)
    col = lax.broadcasted_iota(I32, (tq, tq), 1)
    hs = lambda h: slice(h * HEAD_DIM, (h + 1) * HEAD_DIM)

    off = pl.multiple_of(j * tq, tq)
    for h in range(heads):
        contrib, carry = _sb_block(q_ref[:, hs(h)], k_ref[pl.ds(off, tq), hs(h)], v_ref[pl.ds(off, tq), hs(h)],
                                   u, col < row, jnp.zeros((tq, 1), F32))
        acc_ref[h] = contrib
        carry_ref[h] = carry

    def cond(c):
        return (c >= 0) & (jnp.max(carry_ref[...]) > SB_DEAD)

    def body(c):
        off = pl.multiple_of(c * tq, tq)
        for h in range(heads):
            contrib, carry = _sb_block(q_ref[:, hs(h)], k_ref[pl.ds(off, tq), hs(h)],
                                       v_ref[pl.ds(off, tq), hs(h)], u, None, carry_ref[h])
            acc_ref[h] += contrib
            carry_ref[h] = carry
        return c - 1

    lax.while_loop(cond, body, j - 1)
    for h in range(heads):
        o_ref[:, hs(h)] = acc_ref[h]


def _sb_prompt(qb, kvbb, u_strict, *, nb, s, tq, heads):
    nq = s // tq
    m = nb * s
    w = heads * HEAD_DIM
    ng = H_B // heads
    return pl.pallas_call(
        functools.partial(_sb_prompt_kernel, heads=heads),
        grid=(nb, ng, nq),
        in_specs=[pl.BlockSpec((tq, w), lambda b, g, j: (b * nq + j, g)),
                  pl.BlockSpec((s, w), lambda b, g, j: (b, g)),
                  pl.BlockSpec((s, w), lambda b, g, j: (b, ng + g)),
                  pl.BlockSpec((tq, tq), lambda b, g, j: (0, 0))],
        out_specs=pl.BlockSpec((tq, w), lambda b, g, j: (b * nq + j, g)),
        out_shape=jax.ShapeDtypeStruct((m, W_B), F32),
        scratch_shapes=[pltpu.VMEM((heads, tq, HEAD_DIM), F32), pltpu.VMEM((heads, tq, 1), F32)],
        compiler_params=_cparams(3),
        name="sb_prompt",
    )(qb, kvbb, kvbb, u_strict)


def _merge_kernel(oa_ref, ob_ref, ga_ref, gb_ref, wo_ref, x_ref, gpost_ref, gate_ref,
                  gpre_ref, sh_ref, sc_ref, x1_ref, h2_ref):
    na = _rms(oa_ref[...], ga_ref[...]).astype(BF16)
    nb = _rms(ob_ref[...], gb_ref[...]).astype(BF16)
    o = (jnp.dot(na, wo_ref[:W_A, :], preferred_element_type=F32)
         + jnp.dot(nb, wo_ref[W_A:, :], preferred_element_type=F32))
    x1 = x_ref[...] + gate_ref[...] * _rms(o, gpost_ref[...])
    x1_ref[...] = x1
    h2_ref[...] = (_rms(x1, gpre_ref[...]) * (1.0 + sc_ref[...]) + sh_ref[...]).astype(BF16)


def _merge(oa, ob, g_a, g_b, w_out, x2d, g_post, gate, g_pre_ffn, shift, scale, *, tm, rows_per_group):
    m, d = x2d.shape
    bpg = rows_per_group // tm
    r = gate.shape[1]
    row = lambda i: (i, 0)
    grp = lambda i: (i // bpg, 0, 0)
    one = lambda i: (0, 0)
    return pl.pallas_call(
        _merge_kernel,
        grid=(m // tm,),
        in_specs=[pl.BlockSpec((tm, W_A), row), pl.BlockSpec((tm, W_B), row),
                  pl.BlockSpec((1, W_A), one), pl.BlockSpec((1, W_B), one),
                  pl.BlockSpec((W_A + W_B, d), one),
                  pl.BlockSpec((tm, d), row), pl.BlockSpec((1, d), one),
                  pl.BlockSpec((None, r, d), grp), pl.BlockSpec((1, d), one),
                  pl.BlockSpec((None, r, d), grp), pl.BlockSpec((None, r, d), grp)],
        out_specs=[pl.BlockSpec((tm, d), row), pl.BlockSpec((tm, d), row)],
        out_shape=[jax.ShapeDtypeStruct((m, d), F32), jax.ShapeDtypeStruct((m, d), BF16)],
        compiler_params=_cparams(1),
        name="merge_outproj",
    )(oa, ob, g_a, g_b, w_out, x2d, g_post, gate, g_pre_ffn, shift, scale)


def _ffn_kernel(h_ref, wg_ref, wu_ref, wd_ref, x1_ref, gpost_ref, gate_ref, y_ref, act_ref, f_ref, *, nf, nd):
    s = pl.program_id(1)

    @pl.when(s < nf)
    def _():
        h = h_ref[...]
        g = jnp.dot(h, wg_ref[...], preferred_element_type=F32)
        u = jnp.dot(h, wu_ref[...], preferred_element_type=F32)
        act_ref[s] = (g * jax.nn.sigmoid(g) * u).astype(BF16)

    @pl.when(s >= nf)
    def _():
        act = jnp.concatenate([act_ref[f] for f in range(nf)], axis=1)
        f_ref[s - nf] = jnp.dot(act, wd_ref[...], preferred_element_type=F32)

    @pl.when(s == nf + nd - 1)
    def _():
        fo = jnp.concatenate([f_ref[n] for n in range(nd)], axis=1)
        y_ref[...] = x1_ref[...] + gate_ref[...] * _rms(fo, gpost_ref[...])


def _ffn(h2, w_gu, w_down, x1, g_post, gate, *, tm, tf, td, rows_per_group):
    m, d = x1.shape
    ff = w_down.shape[0]
    nf = ff // tf
    nd = d // td
    bpg = rows_per_group // tm
    r = gate.shape[1]
    up_tile = lambda i, s: (0, jnp.minimum(s, nf - 1))
    return pl.pallas_call(
        functools.partial(_ffn_kernel, nf=nf, nd=nd),
        grid=(m // tm, nf + nd),
        in_specs=[pl.BlockSpec((tm, d), lambda i, s: (i, 0)),
                  pl.BlockSpec((d, tf), up_tile),
                  pl.BlockSpec((d, tf), lambda i, s: (0, nf + jnp.minimum(s, nf - 1))),
                  pl.BlockSpec((ff, td), lambda i, s: (0, jnp.maximum(s - nf, 0))),
                  pl.BlockSpec((tm, d), lambda i, s: (i, 0)),
                  pl.BlockSpec((1, d), lambda i, s: (0, 0)),
                  pl.BlockSpec((None, r, d), lambda i, s: (i // bpg, 0, 0))],
        out_specs=pl.BlockSpec((tm, d), lambda i, s: (i, 0)),
        out_shape=jax.ShapeDtypeStruct((m, d), F32),
        scratch_shapes=[pltpu.VMEM((nf, tm, tf), BF16), pltpu.VMEM((nd, tm, td), F32)],
        compiler_params=_cparams(2),
        name="swiglu_ffn",
    )(h2, w_gu, w_gu, w_down, x1, g_post, gate)


def _page_copy(src_ref, buf_ref, sem_ref, pt_ref, b, p, slot):
    return pltpu.make_async_copy(src_ref.at[pt_ref[b, p]], buf_ref.at[slot, p], sem_ref.at[slot])


def _sample_select_kernel(pt_ref, cidx_ref, q_ref, wsel_ref, knew_ref, ptcol_ref, lstrict_ref, ut_ref, eye_ref,
                          idx_ref, k3_ref, pbuf_ref, sem_ref, *, topk, t_new):
    b = pl.program_id(0)
    nb = pl.num_programs(0)
    n_pages = pbuf_ref.shape[1]
    n_rows = k3_ref.shape[1]
    slot = b % 2
    group = 16

    def start_all(bb, sl):
        def body(p, c):
            _page_copy(cidx_ref, pbuf_ref, sem_ref, pt_ref, bb, p, sl).start()
            return c
        lax.fori_loop(0, n_pages, body, 0)

    @pl.when(b == 0)
    def _():
        start_all(0, 0)

    @pl.when(b + 1 < nb)
    def _():
        start_all(b + 1, 1 - slot)

    def wait_body(p, c):
        _page_copy(cidx_ref, pbuf_ref, sem_ref, pt_ref, b, p, slot).wait()
        return c
    lax.fori_loop(0, n_pages, wait_body, 0)

    q = q_ref[...]
    wsel = wsel_ref[...]
    trow = lax.broadcasted_iota(I32, (8, LANES), 0)

    def to_keys(kt):
        dots = jnp.maximum(jnp.dot(q, kt, preferred_element_type=F32), 0.0)
        sc = jnp.dot(wsel, dots.astype(BF16), preferred_element_type=F32)
        return _float_key(sc)

    def score_group(gi, c):
        base = gi * group
        kt = jnp.concatenate([pbuf_ref[slot, base + u] for u in range(group)], axis=1).astype(BF16)
        keys = to_keys(kt)
        for t in range(t_new):
            for u in range(group):
                k3_ref[t, pl.ds(base + u, 1), :] = keys[t:t + 1, u * LANES:(u + 1) * LANES]
        return c
    lax.fori_loop(0, n_pages // group, score_group, 0)

    knew = to_keys(knew_ref[...])
    lane = lax.broadcasted_iota(I32, (8, LANES), 1)
    knew = jnp.where((lane <= trow) & (lane < t_new), knew, INT_MIN)
    pad = jnp.full((n_rows - n_pages - 1, LANES), INT_MIN, I32)
    for t in range(t_new):
        k3_ref[t, n_pages:n_pages + 1, :] = knew[t:t + 1]
        k3_ref[t, n_pages + 1:, :] = pad

    def search_cond(s):
        it = s[0]
        live = s[1] < s[1 + t_new]
        for t in range(1, t_new):
            live = live | (s[1 + t] < s[1 + t_new + t])
        return (it < 40) & live

    def search_body(s):
        it = s[0]
        los, his = [], []
        for t in range(t_new):
            lo, hi = s[1 + t], s[1 + t_new + t]
            mid = _ceil_avg(lo, hi)
            c = jnp.sum(jnp.where(k3_ref[t] >= mid, 1, 0))
            ge = c >= topk
            los.append(jnp.where(ge, mid, lo))
            his.append(jnp.where(c == topk, mid, jnp.where(ge, hi, mid - 1)))
        return (it + 1, *los, *his)

    init = (jnp.int32(0),) + (jnp.int32(INT_MIN + 1),) * t_new + (jnp.int32(INT_MAX),) * t_new
    res = lax.while_loop(search_cond, search_body, init)

    rowid = lax.broadcasted_iota(I32, (n_rows, 1), 0).astype(F32)
    qlane = lax.broadcasted_iota(I32, (1, 2 * LANES), 1).astype(F32)
    liota = lax.broadcasted_iota(I32, (LANES, 2 * LANES), 0).astype(F32)
    lstrict = lstrict_ref[...]
    idx_ref[...] = jnp.zeros(idx_ref.shape, I32)
    past = lax.broadcasted_iota(I32, (n_rows, 1), 0) < n_pages
    kpos = (lax.broadcasted_iota(I32, (n_rows, LANES), 0) * LANES
            + lax.broadcasted_iota(I32, (n_rows, LANES), 1))
    for t in range(t_new):
        thr = res[1 + t]

        n_gt = jnp.sum(jnp.where(k3_ref[t] > thr, 1, 0))

        @pl.when(jnp.sum(jnp.where(k3_ref[t] >= thr, 1, 0)) > topk)
        def _():
            keep = topk - n_gt

            def pos_body(_, s):
                lo, hi = s
                mid = (lo + hi) >> 1
                ok = jnp.sum(jnp.where((k3_ref[t] == thr) & (kpos <= mid), 1, 0)) >= keep
                return jnp.where(ok, lo, mid + 1), jnp.where(ok, mid, hi)

            last, _ = lax.fori_loop(0, (n_rows * LANES - 1).bit_length(), pos_body,
                                    (jnp.int32(0), jnp.int32(n_rows * LANES - 1)))
            kt = k3_ref[t]
            k3_ref[t] = jnp.where((kt == thr) & (kpos > last), thr - 1, kt)

        sel = jnp.where((k3_ref[t] >= thr) & past, 1.0, 0.0)
        selb = sel.astype(BF16)
        cnt = jnp.sum(sel, axis=1, keepdims=True)
        start = jnp.dot(lstrict, jnp.broadcast_to(cnt, (n_rows, LANES)).astype(BF16),
                        preferred_element_type=F32)[:, :1]
        onehot = jnp.where((start <= qlane) & (qlane < start + cnt), 1.0, 0.0)
        nt = (((1,), (1,)), ((), ()))
        et = lax.dot_general(ut_ref[...], selb, nt, preferred_element_type=F32)
        selt = lax.dot_general(eye_ref[...], selb, nt, preferred_element_type=F32)
        ept = jnp.where(selt > 0.5, et, -1.0).astype(BF16)
        gt = jnp.dot(ept, onehot.astype(BF16), preferred_element_type=F32)
        start_q = jnp.sum(onehot * start, axis=0, keepdims=True)
        chunk_q = jnp.sum(onehot * rowid, axis=0, keepdims=True)
        lane_q = jnp.sum(jnp.where(gt == qlane - start_q, liota, 0.0), axis=0, keepdims=True)
        total = jnp.sum(cnt)
        ids = (chunk_q * LANES + lane_q).astype(I32)
        idx_ref[t:t + 1, :] = jnp.where(qlane < total, ids, -1)
        page_q = jnp.sum(onehot * ptcol_ref[...], axis=0, keepdims=True)
        src = ((page_q * PAGE + lane_q) * (2 * HKV_A)).astype(I32)
        idx_ref[8 + t:9 + t, :] = jnp.where(qlane < total, src, 0)
        new_sel = jnp.where(k3_ref[t, n_pages:n_pages + 1, :] >= thr, 1, 0)
        idx_ref[4 + t:5 + t, :] = jnp.concatenate([new_sel, jnp.zeros_like(new_sel)], axis=1)


def _sample_select(page_table, cidx_t, q_s, wsel, knew_t, *, topk, t_new):
    db, n_pages = page_table.shape
    n_rows = ((n_pages + 1 + LANES - 1) // LANES) * LANES
    lstrict = jnp.asarray(np.tril(np.ones((n_rows, n_rows), np.float32), -1), BF16)
    ut = jnp.asarray(np.tril(np.ones((LANES, LANES), np.float32), -1), BF16)
    eye = jnp.asarray(np.eye(LANES, dtype=np.float32), BF16)
    ptcol = jnp.pad(page_table, ((0, 0), (0, n_rows - n_pages))).astype(F32)[:, :, None]
    const = lambda shape: pl.BlockSpec(shape, lambda b, pt: (0,) * len(shape))
    return pl.pallas_call(
        functools.partial(_sample_select_kernel, topk=topk, t_new=t_new),
        grid_spec=pltpu.PrefetchScalarGridSpec(
            num_scalar_prefetch=1, grid=(db,),
            in_specs=[pl.BlockSpec(memory_space=pl.ANY),
                      pl.BlockSpec((None,) + q_s.shape[1:], lambda b, pt: (b, 0, 0)),
                      pl.BlockSpec((None,) + wsel.shape[1:], lambda b, pt: (b, 0, 0)),
                      pl.BlockSpec((None,) + knew_t.shape[1:], lambda b, pt: (b, 0, 0)),
                      pl.BlockSpec((None, n_rows, 1), lambda b, pt: (b, 0, 0)),
                      const((n_rows, n_rows)), const((LANES, LANES)), const((LANES, LANES))],
            out_specs=pl.BlockSpec((None, 16, 2 * LANES), lambda b, pt: (b, 0, 0)),
            scratch_shapes=[pltpu.VMEM((t_new, n_rows, LANES), I32),
                            pltpu.VMEM((2, n_pages, D_IDX, PAGE), F32),
                            pltpu.SemaphoreType.DMA((2,))]),
        out_shape=jax.ShapeDtypeStruct((db, 16, 2 * LANES), I32),
        compiler_params=_cparams(1),
        name="sample_select",
    )(page_table, cidx_t, q_s, wsel, knew_t, ptcol, lstrict, ut, eye)


def _sample_attend_kernel(src_s_ref, rb_ref, kva_ref, kvnew_ref, idxv_ref, q_ref, o_ref,
                          gbuf_ref, nbuf_ref, sem_ref, *, t_new, n_past):
    b = pl.program_id(0)
    nsel = idxv_ref.shape[1]
    rows = 2 * HKV_A
    unroll = 8

    new_rows = t_new * rows
    cp_new = pltpu.make_async_copy(kvnew_ref.at[pl.ds(b * new_rows, new_rows)],
                                   nbuf_ref.at[pl.ds(0, new_rows)], sem_ref.at[1])
    cp_new.start()
    nbuf_ref[pl.ds(new_rows, nbuf_ref.shape[0] - new_rows), :] = jnp.zeros(
        (nbuf_ref.shape[0] - new_rows, HEAD_DIM), F32)

    def row_copy(t, qi):
        src = src_s_ref[b * t_new + t, qi]
        return pltpu.make_async_copy(kva_ref.at[pl.ds(pl.multiple_of(src, rows), rows)],
                                     gbuf_ref.at[t, pl.ds(pl.multiple_of(qi * rows, rows), rows)],
                                     sem_ref.at[0])

    for t in range(t_new):
        def issue(qo, c):
            for u in range(unroll):
                row_copy(t, qo * unroll + u).start()
            return c
        lax.fori_loop(0, nsel // unroll, issue, 0)
    for t in range(t_new):
        def drain(qo, c):
            for u in range(unroll):
                row_copy(t, qo * unroll + u).wait()
            return c
        lax.fori_loop(0, nsel // unroll, drain, 0)
    cp_new.wait()

    nlane = lax.broadcasted_iota(I32, (1, LANES), 1)
    zpad = jnp.zeros((LANES - 8, HEAD_DIM), BF16)
    nt = (((1,), (1,)), ((), ()))
    for t in range(t_new):
        ids = idxv_ref[t:t + 1, :]
        valid = ids >= 0
        dist = (n_past + t) - ids
        new_valid = (idxv_ref[4 + t:5 + t, :LANES] > 0) & (nlane < t_new)
        new_dist = t - nlane
        for g in range(HKV_A):
            k = gbuf_ref[t, pl.ds(g, nsel, stride=rows), :].astype(BF16)
            v = gbuf_ref[t, pl.ds(HKV_A + g, nsel, stride=rows), :].astype(BF16)
            kn = jnp.concatenate([nbuf_ref[pl.ds(g, 8, stride=rows), :].astype(BF16), zpad], axis=0)
            vn = jnp.concatenate([nbuf_ref[pl.ds(HKV_A + g, 8, stride=rows), :].astype(BF16), zpad], axis=0)
            qg = q_ref[t, g]
            srow = lax.broadcasted_iota(I32, (8, 1), 0)
            s = lax.dot_general(qg, k, nt, preferred_element_type=F32) * SCALE
            s = s + jnp.where(srow == 0, _rel_bias_of(dist, rb_ref, 2 * g), _rel_bias_of(dist, rb_ref, 2 * g + 1))
            s = jnp.where(valid, s, -jnp.inf)
            sn = lax.dot_general(qg, kn, nt, preferred_element_type=F32) * SCALE
            sn = sn + jnp.where(srow == 0, _rel_bias_of(new_dist, rb_ref, 2 * g),
                                _rel_bias_of(new_dist, rb_ref, 2 * g + 1))
            sn = jnp.where(new_valid, sn, -jnp.inf)
            mx = jnp.maximum(jnp.max(s, axis=1, keepdims=True), jnp.max(sn, axis=1, keepdims=True))
            p = jnp.exp(s - mx)
            pn = jnp.exp(sn - mx)
            den = jnp.sum(p, axis=1, keepdims=True) + jnp.sum(pn, axis=1, keepdims=True)
            p = p / den
            pn = pn / den
            out = (jnp.dot(p.astype(BF16), v, preferred_element_type=F32)
                   + jnp.dot(pn.astype(BF16), vn, preferred_element_type=F32))
            o_ref[t:t + 1, (2 * g) * HEAD_DIM:(2 * g + 1) * HEAD_DIM] = out[0:1]
            o_ref[t:t + 1, (2 * g + 1) * HEAD_DIM:(2 * g + 2) * HEAD_DIM] = out[1:2]
    o_ref[t_new:, :] = jnp.zeros((8 - t_new, W_A), F32)


def _sample_attend(idx, rel_bias, kva_rows, kvnew_rows, q_pad, *, t_new, n_past):
    db, _, nsel = idx.shape
    src_s = idx[:, 8:8 + t_new].reshape(db * t_new, nsel)
    return pl.pallas_call(
        functools.partial(_sample_attend_kernel, t_new=t_new, n_past=n_past),
        grid_spec=pltpu.PrefetchScalarGridSpec(
            num_scalar_prefetch=2, grid=(db,),
            in_specs=[pl.BlockSpec(memory_space=pl.ANY), pl.BlockSpec(memory_space=pl.ANY),
                      pl.BlockSpec((None, 16, nsel), lambda b, *_: (b, 0, 0)),
                      pl.BlockSpec((None,) + q_pad.shape[1:], lambda b, *_: (b, 0, 0, 0, 0))],
            out_specs=pl.BlockSpec((None, 8, W_A), lambda b, *_: (b, 0, 0)),
            scratch_shapes=[pltpu.VMEM((t_new, nsel * 2 * HKV_A, HEAD_DIM), F32),
                            pltpu.VMEM((8 * 2 * HKV_A, HEAD_DIM), F32),
                            pltpu.SemaphoreType.DMA((2,))]),
        out_shape=jax.ShapeDtypeStruct((db, 8, W_A), F32),
        compiler_params=_cparams(1),
        name="sample_attend",
    )(src_s, rel_bias, kva_rows, kvnew_rows, idx, q_pad)


def _sb_sample_kernel(pt_ref, kvb_ref, kvnew_ref, q_ref, u_ref, o_ref, pbuf_ref, acc_ref, carry_ref,
                      sem_ref, *, t_new):
    b = pl.program_id(0)
    n_pages = pt_ref.shape[1]
    prow = pbuf_ref.shape[1]
    step = 2 * H_B
    u = u_ref[...]
    trow = lax.broadcasted_iota(I32, (8, PAGE), 0)
    scol = lax.broadcasted_iota(I32, (8, PAGE), 1)
    live_rows = lax.broadcasted_iota(I32, (8, 1), 0) < t_new

    def page_copy(p, slot):
        return pltpu.make_async_copy(kvb_ref.at[pl.ds(pt_ref[b, p] * prow, prow)], pbuf_ref.at[slot],
                                     sem_ref.at[slot])

    def process(slot, mask):
        for h in range(H_B):
            k = pbuf_ref[slot, pl.ds(h, PAGE, stride=step), :].astype(BF16)
            v = pbuf_ref[slot, pl.ds(H_B + h, PAGE, stride=step), :].astype(BF16)
            contrib, carry = _sb_block(q_ref[h], k, v, u, mask, carry_ref[h])
            acc_ref[h] += contrib
            carry_ref[h] = carry

    page_copy(n_pages - 1, 0).start()
    new_rows = t_new * step
    cp_new = pltpu.make_async_copy(kvnew_ref.at[pl.ds(b * new_rows, new_rows)],
                                   pbuf_ref.at[1, pl.ds(0, new_rows)], sem_ref.at[1])
    cp_new.start()
    pbuf_ref[1, pl.ds(new_rows, prow - new_rows), :] = jnp.zeros((prow - new_rows, HEAD_DIM), F32)
    acc_ref[...] = jnp.zeros(acc_ref.shape, F32)
    carry_ref[...] = jnp.zeros(carry_ref.shape, F32)
    cp_new.wait()
    process(1, (scol < trow) & (scol < t_new))

    def alive():
        c = jnp.where(live_rows, carry_ref[...], -jnp.inf)
        return jnp.max(c) > SB_DEAD

    def cond(s):
        i, go = s
        return (i < n_pages) & go

    def body(s):
        i, _ = s
        slot = i % 2
        page_copy(n_pages - 1 - i, slot).wait()

        @pl.when(i + 1 < n_pages)
        def _():
            page_copy(n_pages - 2 - i, 1 - slot).start()
        process(slot, None)
        return i + 1, alive()

    i_end, _ = lax.while_loop(cond, body, (jnp.int32(0), alive()))

    @pl.when(i_end < n_pages)
    def _():
        page_copy(n_pages - 1 - i_end, i_end % 2).wait()

    for h in range(H_B):
        o_ref[:, h * HEAD_DIM:(h + 1) * HEAD_DIM] = acc_ref[h]


def _sb_sample(page_table, kvb_rows, kvnew_rows, q_pad, u_strict, *, t_new):
    db = page_table.shape[0]
    prow = PAGE * 2 * H_B
    return pl.pallas_call(
        functools.partial(_sb_sample_kernel, t_new=t_new),
        grid_spec=pltpu.PrefetchScalarGridSpec(
            num_scalar_prefetch=1, grid=(db,),
            in_specs=[pl.BlockSpec(memory_space=pl.ANY), pl.BlockSpec(memory_space=pl.ANY),
                      pl.BlockSpec((None, H_B, 8, HEAD_DIM), lambda b, pt: (b, 0, 0, 0)),
                      pl.BlockSpec((PAGE, PAGE), lambda b, pt: (0, 0))],
            out_specs=pl.BlockSpec((None, 8, W_B), lambda b, pt: (b, 0, 0)),
            scratch_shapes=[pltpu.VMEM((2, prow, HEAD_DIM), F32),
                            pltpu.VMEM((H_B, 8, HEAD_DIM), F32),
                            pltpu.VMEM((H_B, 8, 1), F32),
                            pltpu.SemaphoreType.DMA((2,))]),
        out_shape=jax.ShapeDtypeStruct((db, 8, W_B), F32),
        compiler_params=_cparams(1),
        name="sb_sample",
    )(page_table, kvb_rows, kvnew_rows, q_pad, u_strict)


def _strict_upper(n):
    return jnp.asarray(np.tril(np.ones((n, n), np.float32), -1), BF16)


def kernel(x_prompt, x_sample, cache_kv_a, cache_idx, cache_kv_b, page_table, c_prompt, c_sample, rel_bias, w_ada, b_ada, g_pre_mix, w_in, g_grp_a, g_grp_b, w_out, g_post_mix, g_pre_ffn, w_gate_up, w_down, g_post_ffn):
    nb, s, d = x_prompt.shape
    db, t_new, _ = x_sample.shape
    n_pages = page_table.shape[1]
    n_past = n_pages * PAGE
    assert w_ada.shape[0] == 1 and d == (H_A + H_B) * HEAD_DIM
    assert cache_kv_a.shape[2:] == (PAGE, 2, HKV_A, HEAD_DIM) and t_new <= 4

    sp = np.cumsum([0, W_A, KV_A, KV_A, H_IDX * D_IDX, D_IDX, H_IDX, W_B, W_B, W_B])
    wi = w_in[0]
    col = lambda i: wi[:, sp[i]:sp[i + 1]]
    w_main = jnp.concatenate([col(0), col(1), col(2), col(3), col(6), col(7), col(8)], axis=1).astype(BF16)
    w_small = jnp.concatenate([col(4), col(5), jnp.zeros((d, LANES - D_IDX - H_IDX), F32)], axis=1).astype(BF16)
    w_small_t = w_small.T
    w_out_b = w_out[0].astype(BF16)
    w_gu_b = w_gate_up[0].astype(BF16)
    w_down_b = w_down[0].astype(BF16)

    mod = _modulation(jnp.concatenate([c_prompt, c_sample], axis=0), w_ada[0], b_ada)
    mod6 = mod.reshape(nb + db, 6, d).transpose(1, 0, 2)
    mp = [mod6[i, :nb].reshape(nb, 1, d) for i in range(6)]
    ms = [jnp.repeat(mod6[i, nb:], t_new, axis=0)[None] for i in range(6)]

    tq = 256
    topk_p = min(TOPK_MAX, s // 4)
    xp = x_prompt.reshape(nb * s, d)
    (qa, kva, kvab, qi, qb, kvb, kvbb, sm, kt) = _project(
        xp, g_pre_mix, mp[0], mp[1], w_main, w_small, w_small_t, tm=min(512, s), rows_per_group=s)
    bias_tiles = _bias_tiles(rel_bias, tq)
    oa = _dsa_prompt(rel_bias, qi, sm, qa, kvab, bias_tiles, nb=nb, s=s, tq=tq, topk=topk_p)
    ob = _sb_prompt(qb, kvbb, _strict_upper(tq), nb=nb, s=s, tq=tq, heads=8)
    x1, h2 = _merge(oa, ob, g_grp_a, g_grp_b, w_out_b, xp, g_post_mix, mp[2], g_pre_ffn, mp[3], mp[4],
                    tm=512, rows_per_group=s)
    y_prompt = _ffn(h2, w_gu_b, w_down_b, x1, g_post_ffn, mp[5], tm=512, tf=512, td=512, rows_per_group=s)

    ms_rows = db * t_new
    topk_s = min(TOPK_MAX, (n_past + t_new) // 4)
    xs = x_sample.reshape(ms_rows, d)
    (qa_s, kva_s, _, qi_s, qb_s, kvb_s, _, sm_s, kt_s) = _project(
        xs, g_pre_mix, ms[0], ms[1], w_main, w_small, w_small_t, tm=ms_rows, rows_per_group=ms_rows)

    q_s = qi_s.reshape(db, t_new * H_IDX, D_IDX)
    w_s = sm_s[:, D_IDX:D_IDX + H_IDX].reshape(db, t_new, H_IDX)
    wsel = (jnp.eye(8, t_new, dtype=F32)[None, :, :, None] * w_s[:, None]).reshape(db, 8, t_new * H_IDX)
    knew_t = jnp.pad(kt_s[0].reshape(D_IDX, db, t_new).transpose(1, 0, 2),
                     ((0, 0), (0, 0), (0, LANES - t_new))).astype(BF16)
    cidx_t = jnp.swapaxes(cache_idx[0], 1, 2)
    idx = _sample_select(page_table, cidx_t, q_s, wsel.astype(BF16), knew_t, topk=topk_s, t_new=t_new)

    kva_rows = cache_kv_a.reshape(-1, HEAD_DIM)
    kvnew_rows = kva_s
    qa4 = qa_s.reshape(db, t_new, HKV_A, H_A // HKV_A, HEAD_DIM)
    q_pad = jnp.pad(qa4, ((0, 0), (0, 0), (0, 0), (0, 8 - H_A // HKV_A), (0, 0)))
    oa_s = _sample_attend(idx, rel_bias, kva_rows, kvnew_rows, q_pad, t_new=t_new, n_past=n_past)

    kvb_rows = cache_kv_b.reshape(-1, HEAD_DIM)
    kvbnew_rows = kvb_s
    qb_pad = jnp.pad(qb_s.reshape(db, t_new, H_B, HEAD_DIM).transpose(0, 2, 1, 3),
                     ((0, 0), (0, 0), (0, 8 - t_new), (0, 0)))
    ob_s = _sb_sample(page_table, kvb_rows, kvbnew_rows, qb_pad, _strict_upper(PAGE), t_new=t_new)

    oa_s = oa_s[:, :t_new].reshape(ms_rows, W_A)
    ob_s = ob_s[:, :t_new].reshape(ms_rows, W_B)
    x1_s, h2_s = _merge(oa_s, ob_s, g_grp_a, g_grp_b, w_out_b, xs, g_post_mix, ms[2], g_pre_ffn, ms[3],
                        ms[4], tm=ms_rows, rows_per_group=ms_rows)
    y_sample = _ffn(h2_s, w_gu_b, w_down_b, x1_s, g_post_ffn, ms[5], tm=ms_rows, tf=512, td=512,
                    rows_per_group=ms_rows)

    return (y_prompt.reshape(nb, s, d),
            y_sample.reshape(db, t_new, d),
            kva.reshape(1, nb, s, 2, HKV_A, HEAD_DIM),
            jnp.swapaxes(kt, 1, 2)[None],
            kvb.reshape(1, nb, s, 2, H_B, HEAD_DIM),
            kva_s.reshape(1, db, t_new, 2, HKV_A, HEAD_DIM),
            sm_s[:, :D_IDX].reshape(1, db, t_new, D_IDX),
            kvb_s.reshape(1, db, t_new, 2, H_B, HEAD_DIM))
```
